```python
import math
import jax, jax.numpy as jnp
from jax import lax
import numpy as np

D_MODEL = 2048
BATCH = 8
SEQ = 4096
DEPTH = 4

SB_HEADS = 8
SB_HEAD_DIM = 128
SB_WIDTH = SB_HEADS * SB_HEAD_DIM
QUERY_BLOCK = 128
SSM_WIDTH = D_MODEL // 4
SSM_GROUP = 16
SSM_GROUPS = SSM_WIDTH // SSM_GROUP
SSM_STATE = 64
SCAN_CHUNK = 128
DT_MIN = 1e-3
DT_MAX = 1e-1
MEM_TOKENS = 256
MEM_HEADS = 4
MEM_HEAD_DIM = 128
MEM_WIDTH = MEM_HEADS * MEM_HEAD_DIM
N_BRANCHES = 3
IN_WIDTH = 3 * SB_WIDTH + SSM_WIDTH + MEM_WIDTH + N_BRANCHES * D_MODEL
D_FF = ((8 * D_MODEL + 3 * 256 - 1) // (3 * 256)) * 256
DN_ALPHA = (2 * DEPTH) ** 0.25
DN_BETA = (8 * DEPTH) ** -0.25
LN_EPS = 1e-5

kernel_name = "hybrid_sb_s5_mem_deepnorm"


def _layer_norm(x, g, b):
    xf = x.astype(jnp.float32)
    mu = jnp.mean(xf, axis=-1, keepdims=True)
    var = jnp.mean(jnp.square(xf - mu), axis=-1, keepdims=True)
    return ((xf - mu) * lax.rsqrt(var + LN_EPS) * g + b).astype(x.dtype)


def _stick_breaking_attention(q, k, v):
    bsz, seq, heads, dh = q.shape
    scale = dh ** -0.5
    outs = []
    for i in range(seq // QUERY_BLOCK):
        q0 = i * QUERY_BLOCK
        k_end = q0 + QUERY_BLOCK
        z = jnp.einsum('bqhd,bkhd->bhqk', q[:, q0:k_end], k[:, :k_end]).astype(jnp.float32) * scale
        t_idx = q0 + jnp.arange(QUERY_BLOCK)
        s_idx = jnp.arange(k_end)
        causal = s_idx[None, :] < t_idx[:, None]
        log1m = jnp.where(causal, jax.nn.log_sigmoid(-z), 0.0)
        after = lax.cumsum(log1m, axis=3, reverse=True) - log1m
        logw = jnp.where(causal, jax.nn.log_sigmoid(z) + after, -jnp.inf)
        w = jnp.exp(logw).astype(v.dtype)
        outs.append(jnp.einsum('bhqk,bkhd->bqhd', w, v[:, :k_end]))
    return jnp.concatenate(outs, axis=1)


def _ssm_combine(left, right):
    a1, b1 = left
    a2, b2 = right
    return a1 * a2, a2 * b1 + b2


def _s5(u, lam_re, lam_im, log_dt, b_re, b_im, c_re, c_im, d_skip):
    f32 = jnp.float32
    bsz, seq, _ = u.shape
    uf = u.astype(f32)
    lam = lax.complex(lam_re.astype(f32), lam_im.astype(f32))
    dt = jnp.exp(log_dt.astype(f32))[:, None]
    lam_bar = jnp.exp(lam * dt)
    b_bar = ((lam_bar - 1.0) / lam)[..., None] * lax.complex(b_re.astype(f32), b_im.astype(f32))
    c = lax.complex(c_re.astype(f32), c_im.astype(f32))
    n_chunks = seq // SCAN_CHUNK
    u_chunks = uf.reshape(bsz, n_chunks, SCAN_CHUNK, SSM_GROUPS, SSM_GROUP).transpose(1, 0, 2, 3, 4)

    def chunk_step(h_prev, uc):
        bu = jnp.einsum('gpc,btgc->btgp', b_bar, uc.astype(jnp.complex64))
        a = jnp.broadcast_to(lam_bar, bu.shape)
        a_cum, h_loc = lax.associative_scan(_ssm_combine, (a, bu), axis=1)
        h_all = h_loc + a_cum * h_prev[:, None]
        y = jnp.real(jnp.einsum('gcp,btgp->btgc', c, h_all))
        return h_all[:, -1], y

    h0 = jnp.zeros((bsz, SSM_GROUPS, SSM_STATE), jnp.complex64)
    _, ys = lax.scan(chunk_step, h0, u_chunks)
    y = ys.transpose(1, 0, 2, 3, 4).reshape(bsz, seq, SSM_WIDTH)
    return y + d_skip.astype(f32) * uf


def _memory_attention(q, mem, w_kv):
    bsz, seq, _ = q.shape
    kv = jnp.einsum('bmd,de->bme', mem, w_kv)
    k, v = jnp.split(kv, 2, axis=-1)
    qh = q.reshape(bsz, seq, MEM_HEADS, MEM_HEAD_DIM)
    kh = k.reshape(bsz, MEM_TOKENS, MEM_HEADS, MEM_HEAD_DIM)
    vh = v.reshape(bsz, MEM_TOKENS, MEM_HEADS, MEM_HEAD_DIM)
    s = jnp.einsum('blhd,bmhd->bhlm', qh, kh).astype(jnp.float32) * (MEM_HEAD_DIM ** -0.5)
    p = jax.nn.softmax(s, axis=-1).astype(v.dtype)
    return jnp.einsum('bhlm,bmhd->blhd', p, vh).reshape(bsz, seq, MEM_WIDTH)


def setup_inputs(seed: int = 0) -> dict:
    key = jax.random.key(seed)
    ks = jax.random.split(key, 24)
    f32 = jnp.float32

    def nrm(k, shape, scale):
        return jax.random.normal(k, shape, f32) * scale

    n_idx = jnp.arange(SSM_STATE, dtype=f32)
    lam_im_base = jnp.broadcast_to(math.pi * n_idx, (DEPTH, SSM_GROUPS, SSM_STATE))
    return {
        "x": nrm(ks[0], (BATCH, SEQ, D_MODEL), 1.0),
        "mem": nrm(ks[1], (BATCH, MEM_TOKENS, D_MODEL), 1.0),
        "w_in": nrm(ks[2], (DEPTH, D_MODEL, IN_WIDTH), D_MODEL ** -0.5),
        "b_in": nrm(ks[3], (DEPTH, IN_WIDTH), 0.02),
        "sb_w_out": nrm(ks[4], (DEPTH, SB_WIDTH, D_MODEL), SB_WIDTH ** -0.5),
        "ssm_lambda_re": -0.5 + nrm(ks[5], (DEPTH, SSM_GROUPS, SSM_STATE), 0.01),
        "ssm_lambda_im": lam_im_base + nrm(ks[6], (DEPTH, SSM_GROUPS, SSM_STATE), 0.01),
        "ssm_log_dt": jax.random.uniform(ks[7], (DEPTH, SSM_GROUPS), f32, math.log(DT_MIN), math.log(DT_MAX)),
        "ssm_b_re": nrm(ks[8], (DEPTH, SSM_GROUPS, SSM_STATE, SSM_GROUP), (2 * SSM_GROUP) ** -0.5),
        "ssm_b_im": nrm(ks[9], (DEPTH, SSM_GROUPS, SSM_STATE, SSM_GROUP), (2 * SSM_GROUP) ** -0.5),
        "ssm_c_re": nrm(ks[10], (DEPTH, SSM_GROUPS, SSM_GROUP, SSM_STATE), (2 * SSM_STATE) ** -0.5),
        "ssm_c_im": nrm(ks[11], (DEPTH, SSM_GROUPS, SSM_GROUP, SSM_STATE), (2 * SSM_STATE) ** -0.5),
        "ssm_d": nrm(ks[12], (DEPTH, SSM_WIDTH), 1.0),
        "ssm_w_glu": nrm(ks[13], (DEPTH, SSM_WIDTH, 2 * SSM_WIDTH), SSM_WIDTH ** -0.5),
        "ssm_w_out": nrm(ks[14], (DEPTH, SSM_WIDTH, D_MODEL), SSM_WIDTH ** -0.5),
        "mem_w_kv": nrm(ks[15], (DEPTH, D_MODEL, 2 * MEM_WIDTH), D_MODEL ** -0.5),
        "mem_w_out": nrm(ks[16], (DEPTH, MEM_WIDTH, D_MODEL), MEM_WIDTH ** -0.5),
        "w_o": nrm(ks[17], (DEPTH, D_MODEL, D_MODEL), DN_BETA * D_MODEL ** -0.5),
        "ln1_g": 1.0 + nrm(ks[18], (DEPTH, D_MODEL), 0.02),
        "ln1_b": nrm(ks[19], (DEPTH, D_MODEL), 0.02),
        "ffn_w_gate_up": nrm(ks[20], (DEPTH, D_MODEL, 2 * D_FF), D_MODEL ** -0.5),
        "ffn_w_down": nrm(ks[21], (DEPTH, D_FF, D_MODEL), DN_BETA * D_FF ** -0.5),
        "ln2_g": 1.0 + nrm(ks[22], (DEPTH, D_MODEL), 0.02),
        "ln2_b": nrm(ks[23], (DEPTH, D_MODEL), 0.02),
    }


def reference(x, mem, w_in, b_in, sb_w_out, ssm_lambda_re, ssm_lambda_im, ssm_log_dt,
              ssm_b_re, ssm_b_im, ssm_c_re, ssm_c_im, ssm_d, ssm_w_glu, ssm_w_out,
              mem_w_kv, mem_w_out, w_o, ln1_g, ln1_b, ffn_w_gate_up, ffn_w_down,
              ln2_g, ln2_b):
    bsz, seq, _ = x.shape
    splits = [SB_WIDTH, 2 * SB_WIDTH, 3 * SB_WIDTH,
              3 * SB_WIDTH + SSM_WIDTH, 3 * SB_WIDTH + SSM_WIDTH + MEM_WIDTH]
    for l in range(DEPTH):
        proj = jnp.einsum('bld,de->ble', x, w_in[l]) + b_in[l]
        q_sb, k_sb, v_sb, u_ssm, q_mem, gate_logits = jnp.split(proj, splits, axis=-1)

        hs = (bsz, seq, SB_HEADS, SB_HEAD_DIM)
        sb = _stick_breaking_attention(q_sb.reshape(hs), k_sb.reshape(hs), v_sb.reshape(hs))
        p_sb = jnp.einsum('ble,ed->bld', sb.reshape(bsz, seq, SB_WIDTH), sb_w_out[l])

        y = _s5(u_ssm, ssm_lambda_re[l], ssm_lambda_im[l], ssm_log_dt[l], ssm_b_re[l], ssm_b_im[l],
                ssm_c_re[l], ssm_c_im[l], ssm_d[l])
        g = jax.nn.gelu(y).astype(x.dtype)
        glu_a, glu_b = jnp.split(jnp.einsum('ble,ef->blf', g, ssm_w_glu[l]), 2, axis=-1)
        p_ssm = jnp.einsum('ble,ed->bld', glu_a * jax.nn.sigmoid(glu_b), ssm_w_out[l])

        mm = _memory_attention(q_mem, mem, mem_w_kv[l])
        p_mem = jnp.einsum('ble,ed->bld', mm, mem_w_out[l])

        gates = jax.nn.sigmoid(gate_logits.astype(jnp.float32)).astype(x.dtype)
        gates = gates.reshape(bsz, seq, N_BRANCHES, D_MODEL)
        merged = gates[:, :, 0] * p_sb + gates[:, :, 1] * p_ssm + gates[:, :, 2] * p_mem
        mix_out = jnp.einsum('bld,de->ble', merged, w_o[l])
        x = _layer_norm(DN_ALPHA * x + mix_out, ln1_g[l], ln1_b[l])

        gate_up = jnp.einsum('bld,df->blf', x, ffn_w_gate_up[l])
        f_gate, f_up = jnp.split(gate_up, 2, axis=-1)
        ffn_out = jnp.einsum('blf,fd->bld', jax.nn.silu(f_gate) * f_up, ffn_w_down[l])
        x = _layer_norm(DN_ALPHA * x + ffn_out, ln2_g[l], ln2_b[l])
    return x
```

```python
import functools
import math

import jax
import jax.numpy as jnp
from jax import lax
from jax.experimental import pallas as pl
from jax.experimental.pallas import tpu as pltpu

F32 = jnp.float32
BF16 = jnp.bfloat16

SB_HEADS = 8
HEAD_DIM = 128
SB_WIDTH = SB_HEADS * HEAD_DIM
SSM_GROUP = 16
SSM_STATE = 64
MEM_HEADS = 4
MEM_WIDTH = MEM_HEADS * HEAD_DIM
N_BRANCHES = 3
DT_MIN = 1e-3
DT_MAX = 1e-1
LN_EPS = 1e-5

SSM_CHUNK = 16
SSM_GROUPS_PER_STEP = 2

SB_TQ = 256
SB_TK = 128
SB_LOG_CUTOFF = -110.0

VMEM_LIMIT = 56 * 1024 * 1024


def _cparams(sem):
    return pltpu.CompilerParams(dimension_semantics=sem, vmem_limit_bytes=VMEM_LIMIT)


def _sigmoid(x):
    return 1.0 / (1.0 + jnp.exp(-x))


def _proj_kernel(x_ref, w_ref, b_ref, s_ref, o_ref, *, n_plain):
    acc = jnp.dot(x_ref[...], w_ref[...], preferred_element_type=F32)
    acc = (acc + b_ref[...]) * s_ref[...]
    j = pl.program_id(1)

    @pl.when(j < n_plain)
    def _():
        o_ref[...] = acc.astype(o_ref.dtype)

    @pl.when(j >= n_plain)
    def _():
        o_ref[...] = _sigmoid(acc).astype(o_ref.dtype)


def _proj(x, w, b, s, *, tm, tn, n_plain):
    m, k = x.shape
    n = w.shape[1]
    return pl.pallas_call(
        functools.partial(_proj_kernel, n_plain=n_plain),
        grid=(m // tm, n // tn),
        in_specs=[
            pl.BlockSpec((tm, k), lambda i, j: (i, 0)),
            pl.BlockSpec((k, tn), lambda i, j: (0, j)),
            pl.BlockSpec((1, tn), lambda i, j: (0, j)),
            pl.BlockSpec((1, tn), lambda i, j: (0, j)),
        ],
        out_specs=pl.BlockSpec((tm, tn), lambda i, j: (i, j)),
        out_shape=jax.ShapeDtypeStruct((m, n), BF16),
        compiler_params=_cparams(("parallel", "arbitrary")),
        name="proj",
    )(x, w, b, s)


def _sb_kernel(q_ref, k_ref, v_ref, tri_ref, o_ref, acc_ref, carry_ref, *, tq, tk):
    i = pl.program_id(2)
    q = q_ref[...]
    acc_ref[...] = jnp.zeros_like(acc_ref)
    carry_ref[...] = jnp.zeros_like(carry_ref)
    n_diag = tq // tk

    def step(j, masked):
        start = pl.multiple_of(j * tk, tk)
        k = k_ref[pl.ds(start, tk), :]
        v = v_ref[pl.ds(start, tk), :]
        z = lax.dot_general(q, k, (((1,), (1,)), ((), ())), preferred_element_type=F32)
        e = jnp.exp(-jnp.abs(z))
        l1m = -(jnp.maximum(z, 0.0) + jnp.log(1.0 + e))
        if masked:
            t_idx = i * tq + lax.broadcasted_iota(jnp.int32, (tq, tk), 0)
            s_idx = j * tk + lax.broadcasted_iota(jnp.int32, (tq, tk), 1)
            causal = s_idx < t_idx
            l1m = jnp.where(causal, l1m, 0.0)
        hi = l1m.astype(BF16)
        lo = (l1m - hi.astype(F32)).astype(BF16)
        cs = jnp.dot(jnp.concatenate([hi, lo], axis=1), tri_ref[...], preferred_element_type=F32)
        incl = cs[:, :tk]
        tot = cs[:, tk:]
        carry = carry_ref[...]
        w = jnp.exp(z + incl + carry)
        if masked:
            w = jnp.where(causal, w, 0.0)
        acc_ref[...] += jnp.dot(w.astype(BF16), v, preferred_element_type=F32)
        carry_ref[...] = carry + tot

    for d in range(n_diag):
        step(i * n_diag + (n_diag - 1 - d), True)

    def cond(c):
        j, m = c
        return jnp.logical_and(j >= 0, m > SB_LOG_CUTOFF)

    def body(c):
        j, _ = c
        step(j, False)
        return j - 1, jnp.max(carry_ref[...])

    lax.while_loop(cond, body, (i * n_diag - 1, jnp.max(carry_ref[...])))
    o_ref[...] = acc_ref[...].astype(o_ref.dtype)


def _sb_prefix_matrix(tk):
    j = jnp.arange(tk)[:, None]
    s = jnp.arange(tk)[None, :]
    half = jnp.concatenate([(j >= s).astype(F32), jnp.ones((tk, tk), F32)], axis=1)
    return jnp.concatenate([half, half], axis=0).astype(BF16)


def _sb_attention(proj, bsz, seq):
    tq, tk = min(SB_TQ, seq), min(SB_TK, seq)
    nq = seq // tq
    tri = _sb_prefix_matrix(tk)
    return pl.pallas_call(
        functools.partial(_sb_kernel, tq=tq, tk=tk),
        grid=(bsz, SB_HEADS, nq),
        in_specs=[
            pl.BlockSpec((tq, HEAD_DIM), lambda b, h, i: (b * nq + i, h)),
            pl.BlockSpec((seq, HEAD_DIM), lambda b, h, i: (b, SB_HEADS + h)),
            pl.BlockSpec((seq, HEAD_DIM), lambda b, h, i: (b, 2 * SB_HEADS + h)),
            pl.BlockSpec((2 * tk, 2 * tk), lambda b, h, i: (0, 0)),
        ],
        out_specs=pl.BlockSpec((tq, HEAD_DIM), lambda b, h, i: (b * nq + i, h)),
        out_shape=jax.ShapeDtypeStruct((bsz * seq, SB_WIDTH), BF16),
        scratch_shapes=[pltpu.VMEM((tq, HEAD_DIM), F32), pltpu.VMEM((tq, tk), F32)],
        compiler_params=_cparams(("parallel", "parallel", "arbitrary")),
        name="sb_attention",
    )(proj, proj, proj, tri)


def _gelu_tanh(x):
    c = math.sqrt(2.0 / math.pi)
    return 0.5 * x * (1.0 + jnp.tanh(c * (x + 0.044715 * (x * x * x))))


def _ssm_kernel(u_ref, bre_ref, bim_ref, a_ref, kt_ref, cc_ref, o_ref, sre_ref, sim_ref, hp_ref,
                *, n_chunks, bsz):
    u = u_ref[0]
    sre_ref[...] = jnp.dot(u, bre_ref[0], preferred_element_type=F32)
    sim_ref[...] = jnp.dot(u, bim_ref[0], preferred_element_type=F32)
    lanes = sre_ref.shape[1]
    a_re = jnp.broadcast_to(a_ref[0, 0:1, :], (bsz, lanes))
    a_im = jnp.broadcast_to(a_ref[0, 1:2, :], (bsz, lanes))

    def body(n, h):
        h_re, h_im = h
        r = pl.multiple_of(n * bsz, bsz)
        hp_ref[pl.ds(r, bsz), 0:lanes] = h_re
        hp_ref[pl.ds(r, bsz), lanes:2 * lanes] = h_im
        s_re = sre_ref[pl.ds(r, bsz), :]
        s_im = sim_ref[pl.ds(r, bsz), :]
        return (a_re * h_re - a_im * h_im + s_re, a_re * h_im + a_im * h_re + s_im)

    zero = jnp.zeros((bsz, lanes), F32)
    lax.fori_loop(0, n_chunks, body, (zero, zero), unroll=8)
    y = jnp.dot(u, kt_ref[0], preferred_element_type=F32)
    y = y + jnp.dot(hp_ref[...].astype(BF16), cc_ref[0], preferred_element_type=F32)
    o_ref[0] = _gelu_tanh(y).astype(o_ref.dtype)


def _ssm_tables(lam_re, lam_im, log_dt, b_re, b_im, c_re, c_im, d_skip):
    hp = lax.Precision.HIGHEST
    t_len = SSM_CHUNK
    depth, groups, pst = lam_re.shape
    gc = SSM_GROUP
    dt = jnp.exp(log_dt)[..., None]
    ar, ai = lam_re * dt, lam_im * dt
    jj = jnp.arange(t_len + 1, dtype=F32)[:, None, None, None]
    mag = jnp.exp(ar[None] * jj)
    pw_re = mag * jnp.cos(ai[None] * jj)
    pw_im = mag * jnp.sin(ai[None] * jj)
    a1_re, a1_im = pw_re[1], pw_im[1]
    den = lam_re * lam_re + lam_im * lam_im
    nr, ni = a1_re - 1.0, a1_im
    f_re = (nr * lam_re + ni * lam_im) / den
    f_im = (ni * lam_re - nr * lam_im) / den
    bb_re = f_re[..., None] * b_re - f_im[..., None] * b_im
    bb_im = f_re[..., None] * b_im + f_im[..., None] * b_re
    pj_re, pj_im = pw_re[:t_len], pw_im[:t_len]
    cp_re = c_re[None] * pj_re[:, :, :, None, :] - c_im[None] * pj_im[:, :, :, None, :]
    cp_im = c_re[None] * pj_im[:, :, :, None, :] + c_im[None] * pj_re[:, :, :, None, :]
    klag = (jnp.einsum('jdgcp,dgpe->jdgce', cp_re, bb_re, precision=hp)
            - jnp.einsum('jdgcp,dgpe->jdgce', cp_im, bb_im, precision=hp))
    s_i = jnp.arange(t_len)[:, None]
    t_i = jnp.arange(t_len)[None, :]
    lag = jnp.clip(t_i - s_i, 0, t_len - 1)
    kt = klag[lag]
    kt = jnp.where((t_i >= s_i)[:, :, None, None, None, None], kt, 0.0)
    kt = kt.transpose(2, 3, 0, 5, 1, 4)
    eye_t = jnp.eye(t_len, dtype=F32)
    eye_c = jnp.eye(gc, dtype=F32)
    dsk = d_skip.reshape(depth, groups, gc)
    kt = kt + (eye_t[None, None, :, None, :, None] * eye_c[None, None, None, :, None, :]
               * dsk[:, :, None, None, None, :])
    kt = kt.reshape(depth, groups, t_len * gc, t_len * gc)
    rev_re, rev_im = pw_re[:t_len][::-1], pw_im[:t_len][::-1]
    bc_re = rev_re[..., None] * bb_re[None] - rev_im[..., None] * bb_im[None]
    bc_im = rev_re[..., None] * bb_im[None] + rev_im[..., None] * bb_re[None]
    bc_re = bc_re.transpose(1, 2, 0, 4, 3).reshape(depth, groups, t_len * gc, pst)
    bc_im = bc_im.transpose(1, 2, 0, 4, 3).reshape(depth, groups, t_len * gc, pst)
    q_re, q_im = pw_re[1:], pw_im[1:]
    cl_re = c_re[None] * q_re[:, :, :, None, :] - c_im[None] * q_im[:, :, :, None, :]
    cl_im = c_re[None] * q_im[:, :, :, None, :] + c_im[None] * q_re[:, :, :, None, :]
    cc_re = cl_re.transpose(1, 2, 4, 0, 3).reshape(depth, groups, pst, t_len * gc)
    cc_im = (-cl_im).transpose(1, 2, 4, 0, 3).reshape(depth, groups, pst, t_len * gc)
    at_re, at_im = pw_re[t_len], pw_im[t_len]

    gp = SSM_GROUPS_PER_STEP
    ns = groups // gp
    eye_g = jnp.eye(gp, dtype=F32)

    def blockdiag(m):
        r, c = m.shape[2], m.shape[3]
        m = m.reshape(depth, ns, gp, r, c)
        out = m[:, :, :, :, None, :] * eye_g[None, None, :, None, :, None]
        return out.reshape(depth, ns, gp * r, gp * c)

    kt_p = blockdiag(kt).astype(BF16)
    bre_p = blockdiag(bc_re).astype(BF16)
    bim_p = blockdiag(bc_im).astype(BF16)
    cc_p = jnp.concatenate([blockdiag(cc_re), blockdiag(cc_im)], axis=2).astype(BF16)
    a_p = jnp.stack([at_re.reshape(depth, ns, gp * pst), at_im.reshape(depth, ns, gp * pst)], axis=2)
    return kt_p, bre_p, bim_p, cc_p, a_p


def _ssm_branch(u, tabs, bsz, seq):
    kt_p, bre_p, bim_p, cc_p, a_p = tabs
    ns = kt_p.shape[0]
    gp, t_len, gc = SSM_GROUPS_PER_STEP, SSM_CHUNK, SSM_GROUP
    n_chunks = seq // t_len
    rows = n_chunks * bsz
    wid = gp * t_len * gc
    lanes = gp * SSM_STATE
    ur = u.reshape(bsz, n_chunks, t_len, ns, gp, gc).transpose(3, 1, 0, 4, 2, 5).reshape(ns, rows, wid)
    g = pl.pallas_call(
        functools.partial(_ssm_kernel, n_chunks=n_chunks, bsz=bsz),
        grid=(ns,),
        in_specs=[
            pl.BlockSpec((1, rows, wid), lambda p: (p, 0, 0)),
            pl.BlockSpec((1, wid, lanes), lambda p: (p, 0, 0)),
            pl.BlockSpec((1, wid, lanes), lambda p: (p, 0, 0)),
            pl.BlockSpec((1, 2, lanes), lambda p: (p, 0, 0)),
            pl.BlockSpec((1, wid, wid), lambda p: (p, 0, 0)),
            pl.BlockSpec((1, 2 * lanes, wid), lambda p: (p, 0, 0)),
        ],
        out_specs=pl.BlockSpec((1, rows, wid), lambda p: (p, 0, 0)),
        out_shape=jax.ShapeDtypeStruct((ns, rows, wid), BF16),
        scratch_shapes=[pltpu.VMEM((rows, lanes), F32), pltpu.VMEM((rows, lanes), F32),
                        pltpu.VMEM((rows, 2 * lanes), F32)],
        compiler_params=_cparams(("parallel",)),
        name="ssm",
    )(ur, bre_p, bim_p, a_p, kt_p, cc_p)
    return g.reshape(ns, n_chunks, bsz, gp, t_len, gc).transpose(2, 1, 4, 0, 3, 5).reshape(bsz * seq, ns * gp * gc)


def _mem_kernel(q_ref, k_ref, v_ref, o_ref):
    for h in range(MEM_HEADS):
        sl = slice(h * HEAD_DIM, (h + 1) * HEAD_DIM)
        s = lax.dot_general(q_ref[:, sl], k_ref[:, sl], (((1,), (1,)), ((), ())),
                            preferred_element_type=F32)
        p = jnp.exp(s - jnp.max(s, axis=-1, keepdims=True))
        l = jnp.sum(p, axis=-1, keepdims=True)
        o = jnp.dot(p.astype(BF16), v_ref[:, sl], preferred_element_type=F32)
        o_ref[:, sl] = (o / l).astype(o_ref.dtype)


def _mem_attention(proj, kv, bsz, seq, n_mem, q_col_block):
    tq = min(512, seq)
    nq = seq // tq
    return pl.pallas_call(
        _mem_kernel,
        grid=(bsz, nq),
        in_specs=[
            pl.BlockSpec((tq, MEM_WIDTH), lambda b, i: (b * nq + i, q_col_block)),
            pl.BlockSpec((n_mem, MEM_WIDTH), lambda b, i: (b, 0)),
            pl.BlockSpec((n_mem, MEM_WIDTH), lambda b, i: (b, 1)),
        ],
        out_specs=pl.BlockSpec((tq, MEM_WIDTH), lambda b, i: (b * nq + i, 0)),
        out_shape=jax.ShapeDtypeStruct((bsz * seq, MEM_WIDTH), BF16),
        compiler_params=_cparams(("parallel", "arbitrary")),
        name="mem_attention",
    )(proj, kv, kv)


def _layer_norm(xf, g, b):
    mu = jnp.mean(xf, axis=-1, keepdims=True)
    xc = xf - mu
    var = jnp.mean(xc * xc, axis=-1, keepdims=True)
    return xc * lax.rsqrt(var + LN_EPS) * g + b


def _merge_kernel(sb_ref, g_ref, mm_ref, g0_ref, g1_ref, g2_ref, x_ref,
                  wsb_ref, wglu_ref, wsso_ref, wmem_ref, wo_ref, lng_ref, lnb_ref,
                  of_ref, ob_ref, *, alpha, ssm_width):
    p_sb = jnp.dot(sb_ref[...], wsb_ref[...], preferred_element_type=F32)
    glu = jnp.dot(g_ref[...], wglu_ref[...], preferred_element_type=F32)
    gated = glu[:, :ssm_width] * _sigmoid(glu[:, ssm_width:])
    p_ssm = jnp.dot(gated.astype(BF16), wsso_ref[...], preferred_element_type=F32)
    p_mem = jnp.dot(mm_ref[...], wmem_ref[...], preferred_element_type=F32)
    merged = (g0_ref[...].astype(F32) * p_sb + g1_ref[...].astype(F32) * p_ssm
              + g2_ref[...].astype(F32) * p_mem)
    mix = jnp.dot(merged.astype(BF16), wo_ref[...], preferred_element_type=F32)
    y = _layer_norm(alpha * x_ref[...] + mix, lng_ref[...], lnb_ref[...])
    of_ref[...] = y
    ob_ref[...] = y.astype(BF16)


def _const_spec(shape):
    return pl.BlockSpec(shape, lambda i: (0,) * len(shape), pipeline_mode=pl.Buffered(1))


def _merge(sb, g, mm, proj, x, wsb, wglu, wsso, wmem, wo, lng, lnb, *, alpha, gate_col_block, tm):
    m, d = x.shape
    ssm_width = g.shape[1]
    row = lambda w: pl.BlockSpec((tm, w), lambda i: (i, 0))
    gate = lambda k: pl.BlockSpec((tm, d), lambda i: (i, gate_col_block + k))
    return pl.pallas_call(
        functools.partial(_merge_kernel, alpha=alpha, ssm_width=ssm_width),
        grid=(m // tm,),
        in_specs=[row(sb.shape[1]), row(ssm_width), row(mm.shape[1]), gate(0), gate(1), gate(2), row(d),
                  _const_spec(wsb.shape), _const_spec(wglu.shape), _const_spec(wsso.shape),
                  _const_spec(wmem.shape), _const_spec(wo.shape), _const_spec(lng.shape),
                  _const_spec(lnb.shape)],
        out_specs=[row(d), row(d)],
        out_shape=[jax.ShapeDtypeStruct((m, d), F32), jax.ShapeDtypeStruct((m, d), BF16)],
        compiler_params=_cparams(("parallel",)),
        name="merge",
    )(sb, g, mm, proj, proj, proj, x, wsb, wglu, wsso, wmem, wo, lng, lnb)


def _ffn_kernel(xb_ref, x_ref, wg_ref, wu_ref, wd_ref, lng_ref, lnb_ref, of_ref, ob_ref, acc_ref,
                *, alpha):
    j = pl.program_id(1)
    xb = xb_ref[...]
    fg = jnp.dot(xb, wg_ref[...], preferred_element_type=F32)
    fu = jnp.dot(xb, wu_ref[...], preferred_element_type=F32)
    h = (fg * _sigmoid(fg) * fu).astype(BF16)
    part = jnp.dot(h, wd_ref[...], preferred_element_type=F32)

    @pl.when(j == 0)
    def _():
        acc_ref[...] = part

    @pl.when(j > 0)
    def _():
        acc_ref[...] += part

    @pl.when(j == pl.num_programs(1) - 1)
    def _():
        y = _layer_norm(alpha * x_ref[...] + acc_ref[...], lng_ref[...], lnb_ref[...])
        of_ref[...] = y
        ob_ref[...] = y.astype(BF16)


def _ffn(xb, x, wgu, wd, lng, lnb, *, alpha, tm, tf):
    m, d = x.shape
    d_ff = wd.shape[0]
    nf = d_ff // tf
    return pl.pallas_call(
        functools.partial(_ffn_kernel, alpha=alpha),
        grid=(m // tm, nf),
        in_specs=[
            pl.BlockSpec((tm, d), lambda i, j: (i, 0)),
            pl.BlockSpec((tm, d), lambda i, j: (i, 0)),
            pl.BlockSpec((d, tf), lambda i, j: (0, j)),
            pl.BlockSpec((d, tf), lambda i, j: (0, nf + j)),
            pl.BlockSpec((tf, d), lambda i, j: (j, 0)),
            pl.BlockSpec((1, d), lambda i, j: (0, 0)),
            pl.BlockSpec((1, d), lambda i, j: (0, 0)),
        ],
        out_specs=[pl.BlockSpec((tm, d), lambda i, j: (i, 0)), pl.BlockSpec((tm, d), lambda i, j: (i, 0))],
        out_shape=[jax.ShapeDtypeStruct((m, d), F32), jax.ShapeDtypeStruct((m, d), BF16)],
        scratch_shapes=[pltpu.VMEM((tm, d), F32)],
        compiler_params=_cparams(("parallel", "arbitrary")),
        name="ffn",
    )(xb, x, wgu, wgu, wd, lng, lnb)


def _forward(x, mem, w_in, b_in, sb_w_out, ssm_lambda_re, ssm_lambda_im, ssm_log_dt,
             ssm_b_re, ssm_b_im, ssm_c_re, ssm_c_im, ssm_d, ssm_w_glu, ssm_w_out,
             mem_w_kv, mem_w_out, w_o, ln1_g, ln1_b, ffn_w_gate_up, ffn_w_down,
             ln2_g, ln2_b, *, alpha):
    bsz, seq, d = x.shape
    depth = w_in.shape[0]
    n_mem = mem.shape[1]
    in_width = w_in.shape[2]
    ssm_width = ssm_d.shape[1]
    m = bsz * seq
    tn = 1024
    tm = min(1024, m)
    q_scale = HEAD_DIM ** -0.5

    u_off = 3 * SB_WIDTH
    qm_off = u_off + ssm_width
    gate_off = qm_off + MEM_WIDTH
    assert gate_off % tn == 0 and gate_off % d == 0 and qm_off % MEM_WIDTH == 0
    assert in_width == gate_off + N_BRANCHES * d
    col = jnp.arange(in_width)
    scale_vec = jnp.where((col < SB_WIDTH) | ((col >= qm_off) & (col < gate_off)), q_scale, 1.0)
    scale_vec = scale_vec.astype(F32)[None, :]

    tabs = _ssm_tables(ssm_lambda_re, ssm_lambda_im, ssm_log_dt, ssm_b_re, ssm_b_im,
                       ssm_c_re, ssm_c_im, ssm_d)
    bf = lambda a: a.astype(BF16)
    w_in_b, wsb_b, wglu_b, wsso_b = bf(w_in), bf(sb_w_out), bf(ssm_w_glu), bf(ssm_w_out)
    wkv_b, wmem_b, wo_b, wgu_b, wd_b = bf(mem_w_kv), bf(mem_w_out), bf(w_o), bf(ffn_w_gate_up), bf(ffn_w_down)
    mem_b = bf(mem.reshape(bsz * n_mem, d))
    kv_zero = jnp.zeros((1, wkv_b.shape[2]), F32)
    kv_one = jnp.ones((1, wkv_b.shape[2]), F32)

    xf = x.reshape(m, d)
    xb = bf(xf)
    for l in range(depth):
        proj = _proj(xb, w_in_b[l], b_in[l][None, :], scale_vec, tm=tm, tn=tn, n_plain=gate_off // tn)
        sb = _sb_attention(proj, bsz, seq)
        g = _ssm_branch(proj[:, u_off:qm_off], tuple(t[l] for t in tabs), bsz, seq)
        kv = _proj(mem_b, wkv_b[l], kv_zero, kv_one, tm=min(1024, bsz * n_mem), tn=wkv_b.shape[2],
                   n_plain=1)
        mm = _mem_attention(proj, kv, bsz, seq, n_mem, qm_off // MEM_WIDTH)
        xf, xb = _merge(sb, g, mm, proj, xf, wsb_b[l], wglu_b[l], wsso_b[l], wmem_b[l], wo_b[l],
                        ln1_g[l][None, :], ln1_b[l][None, :], alpha=alpha,
                        gate_col_block=gate_off // d, tm=min(256, m))
        xf, xb = _ffn(xb, xf, wgu_b[l], wd_b[l], ln2_g[l][None, :], ln2_b[l][None, :],
                      alpha=alpha, tm=min(512, m), tf=512)
    return xf.reshape(bsz, seq, d)


def kernel(x, mem, w_in, b_in, sb_w_out, ssm_lambda_re, ssm_lambda_im, ssm_log_dt, ssm_b_re, ssm_b_im, ssm_c_re, ssm_c_im, ssm_d, ssm_w_glu, ssm_w_out, mem_w_kv, mem_w_out, w_o, ln1_g, ln1_b, ffn_w_gate_up, ffn_w_down, ln2_g, ln2_b):
    depth = w_in.shape[0]
    alpha = (2 * depth) ** 0.25
    return _forward(x, mem, w_in, b_in, sb_w_out, ssm_lambda_re, ssm_lambda_im, ssm_log_dt,
                    ssm_b_re, ssm_b_im, ssm_c_re, ssm_c_im, ssm_d, ssm_w_glu, ssm_w_out,
                    mem_w_kv, mem_w_out, w_o, ln1_g, ln1_b, ffn_w_gate_up, ffn_w_down,
                    ln2_g, ln2_b, alpha=alpha)
```

```python
import functools
import math

import jax
import jax.numpy as jnp
from jax import lax
from jax.experimental import pallas as pl
from jax.experimental.pallas import tpu as pltpu

F32 = jnp.float32
BF16 = jnp.bfloat16

SB_HEADS = 8
HEAD_DIM = 128
SB_WIDTH = SB_HEADS * HEAD_DIM
SSM_GROUP = 16
SSM_STATE = 64
MEM_HEADS = 4
MEM_WIDTH = MEM_HEADS * HEAD_DIM
N_BRANCHES = 3
LN_EPS = 1e-5

LANES = 128

SSM_CHUNK = 16
SSM_ROW_BLOCK = 16
SSM_GROUPS_PER_STEP = LANES // SSM_GROUP

SB_TQ = 256
SB_TK = 128
SB_LOG_CUTOFF = -110.0
SB_STATIC_TILES = 2

VMEM_LIMIT = 56 * 1024 * 1024


def _cparams(sem):
    return pltpu.CompilerParams(dimension_semantics=sem, vmem_limit_bytes=VMEM_LIMIT)


def _sigmoid(x):
    return 1.0 / (1.0 + jnp.exp(-x))


def _proj_kernel(x_ref, w_ref, b_ref, s_ref, o_ref, *, n_plain):
    acc = jnp.dot(x_ref[...], w_ref[...], preferred_element_type=F32)
    acc = (acc + b_ref[...]) * s_ref[...]
    j = pl.program_id(1)

    @pl.when(j < n_plain)
    def _():
        o_ref[...] = acc.astype(o_ref.dtype)

    @pl.when(j >= n_plain)
    def _():
        o_ref[...] = _sigmoid(acc).astype(o_ref.dtype)


def _proj(x, w, b, s, *, tm, tn, n_plain):
    m, k = x.shape
    n = w.shape[1]
    return pl.pallas_call(
        functools.partial(_proj_kernel, n_plain=n_plain),
        grid=(m // tm, n // tn),
        in_specs=[
            pl.BlockSpec((tm, k), lambda i, j: (i, 0)),
            pl.BlockSpec((k, tn), lambda i, j: (0, j)),
            pl.BlockSpec((1, tn), lambda i, j: (0, j)),
            pl.BlockSpec((1, tn), lambda i, j: (0, j)),
        ],
        out_specs=pl.BlockSpec((tm, tn), lambda i, j: (i, j)),
        out_shape=jax.ShapeDtypeStruct((m, n), BF16),
        compiler_params=_cparams(("parallel", "arbitrary")),
        name="proj",
    )(x, w, b, s)


def _sb_kernel(q_ref, k_ref, v_ref, tri_ref, o_ref, acc_ref, carry_ref, *, tq, tk, n_static):
    nq = q_ref.shape[0] // tq
    n_diag = tq // tk

    def tile(i, q, j, carry, masked):
        start = pl.multiple_of(j * tk, tk)
        k = k_ref[pl.ds(start, tk), :]
        v = v_ref[pl.ds(start, tk), :]
        z = lax.dot_general(q, k, (((1,), (1,)), ((), ())), preferred_element_type=F32)
        e = jnp.exp(-jnp.abs(z))
        l1m = -(jnp.maximum(z, 0.0) + jnp.log(1.0 + e))
        if masked:
            t_idx = i * tq + lax.broadcasted_iota(jnp.int32, (tq, tk), 0)
            s_idx = j * tk + lax.broadcasted_iota(jnp.int32, (tq, tk), 1)
            causal = s_idx < t_idx
            l1m = jnp.where(causal, l1m, 0.0)
        hi = l1m.astype(BF16)
        lo = (l1m - hi.astype(F32)).astype(BF16)
        cs = jnp.dot(jnp.concatenate([hi, lo], axis=1), tri_ref[...], preferred_element_type=F32)
        incl = cs[:, :tk]
        tot = cs[:, tk:]
        w = jnp.exp(z + incl + carry)
        if masked:
            w = jnp.where(causal, w, 0.0)
        return jnp.dot(w.astype(BF16), v, preferred_element_type=F32), carry + tot

    def qtile(i, n_st):
        row0 = pl.multiple_of(i * tq, tq)
        q = q_ref[pl.ds(row0, tq), :]
        acc = jnp.zeros((tq, HEAD_DIM), F32)
        carry = jnp.zeros((tq, tk), F32)
        for d in range(n_diag):
            pv, carry = tile(i, q, i * n_diag + (n_diag - 1 - d), carry, True)
            acc = acc + pv
        for s in range(n_st):
            pv, carry = tile(i, q, i * n_diag - 1 - s, carry, False)
            acc = acc + pv
        acc_ref[...] = acc
        carry_ref[...] = carry

        def cond(c):
            j, m = c
            return jnp.logical_and(j >= 0, m > SB_LOG_CUTOFF)

        def body(c):
            j, _ = c
            pv, new_carry = tile(i, q, j, carry_ref[...], False)
            acc_ref[...] += pv
            carry_ref[...] = new_carry
            return j - 1, jnp.max(new_carry)

        lax.while_loop(cond, body, (i * n_diag - 1 - n_st, jnp.max(carry)))
        o_ref[pl.ds(row0, tq), :] = acc_ref[...].astype(o_ref.dtype)

    qtile(0, 0)
    if nq > 1:
        n_st = min(n_static, n_diag)

        def qbody(i, c):
            qtile(i, n_st)
            return c

        lax.fori_loop(1, nq, qbody, 0)


def _sb_prefix_matrix(tk):
    j = jnp.arange(tk)[:, None]
    s = jnp.arange(tk)[None, :]
    half = jnp.concatenate([(j >= s).astype(F32), jnp.ones((tk, tk), F32)], axis=1)
    return jnp.concatenate([half, half], axis=0).astype(BF16)


def _sb_attention(proj, bsz, seq):
    tq, tk = min(SB_TQ, seq), min(SB_TK, seq)
    tri = _sb_prefix_matrix(tk)
    col = lambda off: pl.BlockSpec((seq, HEAD_DIM), lambda b, h: (b, off + h))
    return pl.pallas_call(
        functools.partial(_sb_kernel, tq=tq, tk=tk, n_static=SB_STATIC_TILES),
        grid=(bsz, SB_HEADS),
        in_specs=[col(0), col(SB_HEADS), col(2 * SB_HEADS),
                  pl.BlockSpec((2 * tk, 2 * tk), lambda b, h: (0, 0))],
        out_specs=pl.BlockSpec((seq, HEAD_DIM), lambda b, h: (b, h)),
        out_shape=jax.ShapeDtypeStruct((bsz * seq, SB_WIDTH), BF16),
        scratch_shapes=[pltpu.VMEM((tq, HEAD_DIM), F32), pltpu.VMEM((tq, tk), F32)],
        compiler_params=_cparams(("parallel", "arbitrary")),
        name="sb_attention",
    )(proj, proj, proj, tri)


def _gelu_tanh(x):
    c = math.sqrt(2.0 / math.pi)
    return 0.5 * x * (1.0 + jnp.tanh(c * (x + 0.044715 * (x * x * x))))


def _ssm_kernel(u_ref, bb_ref, cc_ref, a_ref, d_ref, o_ref, x_ref, s_ref, hp_ref, *, t_len, rb):
    n = u_ref.shape[2]
    sl = a_ref.shape[-1]
    for t in range(t_len):
        x_ref[t] = jnp.dot(u_ref[0, t], bb_ref[0], preferred_element_type=F32)

    def in_chunk(from_entering_state):
        def body(r, c):
            r0 = pl.multiple_of(r * rb, rb)
            rows = pl.ds(r0, rb)
            a_re = jnp.broadcast_to(a_ref[0, 0:1, :], (rb, sl))
            a_im = jnp.broadcast_to(a_ref[0, 1:2, :], (rb, sl))
            if from_entering_state:
                h_re, h_im = hp_ref[rows, 0:sl], hp_ref[rows, sl:2 * sl]
            else:
                h_re = h_im = jnp.zeros((rb, sl), F32)
            for t in range(t_len):
                x_re, x_im = x_ref[t, rows, 0:sl], x_ref[t, rows, sl:2 * sl]
                h_re, h_im = (a_re * h_re - a_im * h_im + x_re, a_re * h_im + a_im * h_re + x_im)
                if from_entering_state:
                    x_ref[t, rows, 0:sl] = h_re
                    x_ref[t, rows, sl:2 * sl] = h_im
            if not from_entering_state:
                s_ref[rows, 0:sl] = h_re
                s_ref[rows, sl:2 * sl] = h_im
            return c

        lax.fori_loop(0, n // rb, body, 0)

    in_chunk(False)

    at_re, at_im = a_ref[0, 2:3, :], a_ref[0, 3:4, :]

    def chunk_scan(i, h):
        h_re, h_im = h
        row = pl.ds(i, 1)
        hp_ref[row, 0:sl] = h_re
        hp_ref[row, sl:2 * sl] = h_im
        s_re, s_im = s_ref[row, 0:sl], s_ref[row, sl:2 * sl]
        return (at_re * h_re - at_im * h_im + s_re, at_re * h_im + at_im * h_re + s_im)

    zero = jnp.zeros((1, sl), F32)
    lax.fori_loop(0, n, chunk_scan, (zero, zero), unroll=8)

    in_chunk(True)

    for t in range(t_len):
        y = jnp.dot(x_ref[t].astype(BF16), cc_ref[0], preferred_element_type=F32)
        y = y + d_ref[0] * u_ref[0, t].astype(F32)
        o_ref[0, t] = _gelu_tanh(y).astype(o_ref.dtype)


def _ssm_tables(lam_re, lam_im, log_dt, b_re, b_im, c_re, c_im, d_skip):
    t_len = SSM_CHUNK
    depth, groups, pst = lam_re.shape
    gc, gp = SSM_GROUP, SSM_GROUPS_PER_STEP
    nb = groups // gp
    dt = jnp.exp(log_dt)[..., None]
    ar, ai = lam_re * dt, lam_im * dt
    a_re, a_im = jnp.exp(ar) * jnp.cos(ai), jnp.exp(ar) * jnp.sin(ai)
    at_re = jnp.exp(t_len * ar) * jnp.cos(t_len * ai)
    at_im = jnp.exp(t_len * ar) * jnp.sin(t_len * ai)
    den = lam_re * lam_re + lam_im * lam_im
    nr, ni = a_re - 1.0, a_im
    f_re = (nr * lam_re + ni * lam_im) / den
    f_im = (ni * lam_re - nr * lam_im) / den
    bb_re = f_re[..., None] * b_re - f_im[..., None] * b_im
    bb_im = f_re[..., None] * b_im + f_im[..., None] * b_re
    eye = jnp.eye(gp, dtype=F32)

    def in_block(m):
        m = m.reshape(depth, nb, gp, pst, gc).transpose(0, 1, 2, 4, 3)
        return (m[:, :, :, :, None, :] * eye[None, None, :, None, :, None]).reshape(depth, nb, gp * gc, gp * pst)

    def out_block(m):
        m = m.reshape(depth, nb, gp, gc, pst).transpose(0, 1, 2, 4, 3)
        return (m[:, :, :, :, None, :] * eye[None, None, :, None, :, None]).reshape(depth, nb, gp * pst, gp * gc)

    bb_t = jnp.concatenate([in_block(bb_re), in_block(bb_im)], axis=3).astype(BF16)
    cc_t = jnp.concatenate([out_block(c_re), out_block(-c_im)], axis=2).astype(BF16)
    lanes = lambda m: m.reshape(depth, nb, gp * pst)
    a_t = jnp.stack([lanes(a_re), lanes(a_im), lanes(at_re), lanes(at_im)], axis=2)
    d_t = d_skip.reshape(depth, nb, 1, gp * gc)
    return bb_t, cc_t, a_t, d_t


def _ssm_branch(u, tabs, bsz, seq):
    bb_t, cc_t, a_t, d_t = tabs
    nb, win, sl2 = bb_t.shape
    t_len = SSM_CHUNK
    n = seq // t_len
    rb = min(SSM_ROW_BLOCK, n)
    width = u.shape[1]
    ur = u.reshape(bsz, n, t_len, width).transpose(0, 2, 1, 3)
    blk = pl.BlockSpec((1, t_len, n, win), lambda b, k: (b, 0, 0, k))
    tab = lambda a: pl.BlockSpec((1,) + a.shape[1:], lambda b, k: (k, 0, 0))
    g = pl.pallas_call(
        functools.partial(_ssm_kernel, t_len=t_len, rb=rb),
        grid=(bsz, nb),
        in_specs=[blk, tab(bb_t), tab(cc_t), tab(a_t), tab(d_t)],
        out_specs=blk,
        out_shape=jax.ShapeDtypeStruct((bsz, t_len, n, width), BF16),
        scratch_shapes=[pltpu.VMEM((t_len, n, sl2), F32), pltpu.VMEM((n, sl2), F32),
                        pltpu.VMEM((n, sl2), F32)],
        compiler_params=_cparams(("parallel", "arbitrary")),
        name="ssm",
    )(ur, bb_t, cc_t, a_t, d_t)
    return g.transpose(0, 2, 1, 3).reshape(bsz * seq, width)


def _mem_kernel(q_ref, k_ref, v_ref, o_ref):
    for h in range(MEM_HEADS):
        sl = slice(h * HEAD_DIM, (h + 1) * HEAD_DIM)
        s = lax.dot_general(q_ref[:, sl], k_ref[:, sl], (((1,), (1,)), ((), ())),
                            preferred_element_type=F32)
        p = jnp.exp(s - jnp.max(s, axis=-1, keepdims=True))
        l = jnp.sum(p, axis=-1, keepdims=True)
        o = jnp.dot(p.astype(BF16), v_ref[:, sl], preferred_element_type=F32)
        o_ref[:, sl] = (o / l).astype(o_ref.dtype)


def _mem_attention(proj, kv, bsz, seq, n_mem, q_col_block):
    tq = min(512, seq)
    nq = seq // tq
    return pl.pallas_call(
        _mem_kernel,
        grid=(bsz, nq),
        in_specs=[
            pl.BlockSpec((tq, MEM_WIDTH), lambda b, i: (b * nq + i, q_col_block)),
            pl.BlockSpec((n_mem, MEM_WIDTH), lambda b, i: (b, 0)),
            pl.BlockSpec((n_mem, MEM_WIDTH), lambda b, i: (b, 1)),
        ],
        out_specs=pl.BlockSpec((tq, MEM_WIDTH), lambda b, i: (b * nq + i, 0)),
        out_shape=jax.ShapeDtypeStruct((bsz * seq, MEM_WIDTH), BF16),
        compiler_params=_cparams(("parallel", "arbitrary")),
        name="mem_attention",
    )(proj, kv, kv)


def _layer_norm(xf, g, b):
    mu = jnp.mean(xf, axis=-1, keepdims=True)
    xc = xf - mu
    var = jnp.mean(xc * xc, axis=-1, keepdims=True)
    return xc * lax.rsqrt(var + LN_EPS) * g + b


def _merge_kernel(sb_ref, g_ref, mm_ref, g0_ref, g1_ref, g2_ref, x_ref,
                  wsb_ref, wglu_ref, wsso_ref, wmem_ref, wo_ref, lng_ref, lnb_ref,
                  of_ref, ob_ref, *, alpha, ssm_width):
    p_sb = jnp.dot(sb_ref[...], wsb_ref[...], preferred_element_type=F32)
    glu = jnp.dot(g_ref[...], wglu_ref[...], preferred_element_type=F32)
    gated = glu[:, :ssm_width] * _sigmoid(glu[:, ssm_width:])
    p_ssm = jnp.dot(gated.astype(BF16), wsso_ref[...], preferred_element_type=F32)
    p_mem = jnp.dot(mm_ref[...], wmem_ref[...], preferred_element_type=F32)
    merged = (g0_ref[...].astype(F32) * p_sb + g1_ref[...].astype(F32) * p_ssm
              + g2_ref[...].astype(F32) * p_mem)
    mix = jnp.dot(merged.astype(BF16), wo_ref[...], preferred_element_type=F32)
    y = _layer_norm(alpha * x_ref[...] + mix, lng_ref[...], lnb_ref[...])
    of_ref[...] = y
    ob_ref[...] = y.astype(BF16)


def _const_spec(shape):
    return pl.BlockSpec(shape, lambda i: (0,) * len(shape), pipeline_mode=pl.Buffered(1))


def _merge(sb, g, mm, proj, x, wsb, wglu, wsso, wmem, wo, lng, lnb, *, alpha, gate_col_block, tm):
    m, d = x.shape
    ssm_width = g.shape[1]
    row = lambda w: pl.BlockSpec((tm, w), lambda i: (i, 0))
    gate = lambda k: pl.BlockSpec((tm, d), lambda i: (i, gate_col_block + k))
    return pl.pallas_call(
        functools.partial(_merge_kernel, alpha=alpha, ssm_width=ssm_width),
        grid=(m // tm,),
        in_specs=[row(sb.shape[1]), row(ssm_width), row(mm.shape[1]), gate(0), gate(1), gate(2), row(d),
                  _const_spec(wsb.shape), _const_spec(wglu.shape), _const_spec(wsso.shape),
                  _const_spec(wmem.shape), _const_spec(wo.shape), _const_spec(lng.shape),
                  _const_spec(lnb.shape)],
        out_specs=[row(d), row(d)],
        out_shape=[jax.ShapeDtypeStruct((m, d), F32), jax.ShapeDtypeStruct((m, d), BF16)],
        compiler_params=_cparams(("parallel",)),
        name="merge",
    )(sb, g, mm, proj, proj, proj, x, wsb, wglu, wsso, wmem, wo, lng, lnb)


def _ffn_kernel(xb_ref, x_ref, wg_ref, wu_ref, wd_ref, lng_ref, lnb_ref, of_ref, ob_ref, acc_ref,
                *, alpha):
    j = pl.program_id(1)
    xb = xb_ref[...]
    fg = jnp.dot(xb, wg_ref[...], preferred_element_type=F32)
    fu = jnp.dot(xb, wu_ref[...], preferred_element_type=F32)
    h = (fg * _sigmoid(fg) * fu).astype(BF16)
    part = jnp.dot(h, wd_ref[...], preferred_element_type=F32)

    @pl.when(j == 0)
    def _():
        acc_ref[...] = part

    @pl.when(j > 0)
    def _():
        acc_ref[...] += part

    @pl.when(j == pl.num_programs(1) - 1)
    def _():
        y = _layer_norm(alpha * x_ref[...] + acc_ref[...], lng_ref[...], lnb_ref[...])
        of_ref[...] = y
        ob_ref[...] = y.astype(BF16)


def _ffn(xb, x, wgu, wd, lng, lnb, *, alpha, tm, tf):
    m, d = x.shape
    d_ff = wd.shape[0]
    nf = d_ff // tf
    return pl.pallas_call(
        functools.partial(_ffn_kernel, alpha=alpha),
        grid=(m // tm, nf),
        in_specs=[
            pl.BlockSpec((tm, d), lambda i, j: (i, 0)),
            pl.BlockSpec((tm, d), lambda i, j: (i, 0)),
            pl.BlockSpec((d, tf), lambda i, j: (0, j)),
            pl.BlockSpec((d, tf), lambda i, j: (0, nf + j)),
            pl.BlockSpec((tf, d), lambda i, j: (j, 0)),
            pl.BlockSpec((1, d), lambda i, j: (0, 0)),
            pl.BlockSpec((1, d), lambda i, j: (0, 0)),
        ],
        out_specs=[pl.BlockSpec((tm, d), lambda i, j: (i, 0)), pl.BlockSpec((tm, d), lambda i, j: (i, 0))],
        out_shape=[jax.ShapeDtypeStruct((m, d), F32), jax.ShapeDtypeStruct((m, d), BF16)],
        scratch_shapes=[pltpu.VMEM((tm, d), F32)],
        compiler_params=_cparams(("parallel", "arbitrary")),
        name="ffn",
    )(xb, x, wgu, wgu, wd, lng, lnb)


def _forward(x, mem, w_in, b_in, sb_w_out, ssm_lambda_re, ssm_lambda_im, ssm_log_dt,
             ssm_b_re, ssm_b_im, ssm_c_re, ssm_c_im, ssm_d, ssm_w_glu, ssm_w_out,
             mem_w_kv, mem_w_out, w_o, ln1_g, ln1_b, ffn_w_gate_up, ffn_w_down,
             ln2_g, ln2_b, *, alpha):
    bsz, seq, d = x.shape
    depth = w_in.shape[0]
    n_mem = mem.shape[1]
    in_width = w_in.shape[2]
    ssm_width = ssm_d.shape[1]
    m = bsz * seq
    tn = 1024
    tm = min(1024, m)
    q_scale = HEAD_DIM ** -0.5

    u_off = 3 * SB_WIDTH
    qm_off = u_off + ssm_width
    gate_off = qm_off + MEM_WIDTH
    assert gate_off % tn == 0 and gate_off % d == 0 and qm_off % MEM_WIDTH == 0
    assert in_width == gate_off + N_BRANCHES * d
    col = jnp.arange(in_width)
    scale_vec = jnp.where((col < SB_WIDTH) | ((col >= qm_off) & (col < gate_off)), q_scale, 1.0)
    scale_vec = scale_vec.astype(F32)[None, :]

    tabs = _ssm_tables(ssm_lambda_re, ssm_lambda_im, ssm_log_dt, ssm_b_re, ssm_b_im,
                       ssm_c_re, ssm_c_im, ssm_d)
    bf = lambda a: a.astype(BF16)
    w_in_b, wsb_b, wglu_b, wsso_b = bf(w_in), bf(sb_w_out), bf(ssm_w_glu), bf(ssm_w_out)
    wkv_b, wmem_b, wo_b, wgu_b, wd_b = bf(mem_w_kv), bf(mem_w_out), bf(w_o), bf(ffn_w_gate_up), bf(ffn_w_down)
    mem_b = bf(mem.reshape(bsz * n_mem, d))
    kv_zero = jnp.zeros((1, wkv_b.shape[2]), F32)
    kv_one = jnp.ones((1, wkv_b.shape[2]), F32)

    xf = x.reshape(m, d)
    xb = bf(xf)
    for l in range(depth):
        proj = _proj(xb, w_in_b[l], b_in[l][None, :], scale_vec, tm=tm, tn=tn, n_plain=gate_off // tn)
        sb = _sb_attention(proj, bsz, seq)
        g = _ssm_branch(proj[:, u_off:qm_off], tuple(t[l] for t in tabs), bsz, seq)
        kv = _proj(mem_b, wkv_b[l], kv_zero, kv_one, tm=min(1024, bsz * n_mem), tn=wkv_b.shape[2],
                   n_plain=1)
        mm = _mem_attention(proj, kv, bsz, seq, n_mem, qm_off // MEM_WIDTH)
        xf, xb = _merge(sb, g, mm, proj, xf, wsb_b[l], wglu_b[l], wsso_b[l], wmem_b[l], wo_b[l],
                        ln1_g[l][None, :], ln1_b[l][None, :], alpha=alpha,
                        gate_col_block=gate_off // d, tm=min(256, m))
        xf, xb = _ffn(xb, xf, wgu_b[l], wd_b[l], ln2_g[l][None, :], ln2_b[l][None, :],
                      alpha=alpha, tm=min(512, m), tf=512)
    return xf.reshape(bsz, seq, d)


def kernel(x, mem, w_in, b_in, sb_w_out, ssm_lambda_re, ssm_lambda_im, ssm_log_dt, ssm_b_re, ssm_b_im, ssm_c_re, ssm_c_im, ssm_d, ssm_w_glu, ssm_w_out, mem_w_kv, mem_w_out, w_o, ln1_g, ln1_b, ffn_w_gate_up, ffn_w_down, ln2_g, ln2_b):
    depth = w_in.shape[0]
    alpha = (2 * depth) ** 0.25
    return _forward(x, mem, w_in, b_in, sb_w_out, ssm_lambda_re, ssm_lambda_im, ssm_log_dt,
                    ssm_b_re, ssm_b_im, ssm_c_re, ssm_c_im, ssm_d, ssm_w_glu, ssm_w_out,
                    mem_w_kv, mem_w_out, w_o, ln1_g, ln1_b, ffn_w_gate_up, ffn_w_down,
                    ln2_g, ln2_b, alpha=alpha)
```

```python
import functools
import math

import jax
import jax.numpy as jnp
from jax import lax
from jax.experimental import pallas as pl
from jax.experimental.pallas import tpu as pltpu

F32 = jnp.float32
BF16 = jnp.bfloat16

SB_HEADS = 8
HEAD_DIM = 128
SB_WIDTH = SB_HEADS * HEAD_DIM
SSM_GROUP = 16
SSM_STATE = 64
MEM_HEADS = 4
MEM_WIDTH = MEM_HEADS * HEAD_DIM
N_BRANCHES = 3
LN_EPS = 1e-5

LANES = 128

SSM_CHUNK = 16
SSM_ROW_BLOCK = 16
SSM_GROUPS_PER_STEP = LANES // SSM_GROUP

SB_TQ = 256
SB_TK = 128
SB_LOG_CUTOFF = -110.0
SB_STATIC_TILES = 2
SB_HEADS_PER_STEP = 2

VMEM_LIMIT = 56 * 1024 * 1024


def _cparams(sem):
    return pltpu.CompilerParams(dimension_semantics=sem, vmem_limit_bytes=VMEM_LIMIT)


def _sigmoid(x):
    return 1.0 / (1.0 + jnp.exp(-x))


def _proj_kernel(x_ref, w_ref, b_ref, s_ref, o_ref, *, n_plain):
    acc = jnp.dot(x_ref[...], w_ref[...], preferred_element_type=F32)
    acc = (acc + b_ref[...]) * s_ref[...]
    is_gate = pl.program_id(1) >= n_plain
    o_ref[...] = jnp.where(is_gate, _sigmoid(acc), acc).astype(o_ref.dtype)


def _proj(x, w, b, s, *, tm, tn, n_plain):
    m, k = x.shape
    n = w.shape[1]
    return pl.pallas_call(
        functools.partial(_proj_kernel, n_plain=n_plain),
        grid=(m // tm, n // tn),
        in_specs=[
            pl.BlockSpec((tm, k), lambda i, j: (i, 0)),
            pl.BlockSpec((k, tn), lambda i, j: (0, j)),
            pl.BlockSpec((1, tn), lambda i, j: (0, j)),
            pl.BlockSpec((1, tn), lambda i, j: (0, j)),
        ],
        out_specs=pl.BlockSpec((tm, tn), lambda i, j: (i, j)),
        out_shape=jax.ShapeDtypeStruct((m, n), BF16),
        compiler_params=_cparams(("parallel", "arbitrary")),
        name="proj",
    )(x, w, b, s)


def _sb_kernel(q_ref, k_ref, v_ref, tri_ref, o_ref, acc_ref, carry_ref, *, tq, tk, n_static):
    nq = q_ref.shape[0] // tq
    n_diag = tq // tk
    heads = range(q_ref.shape[1] // HEAD_DIM)
    lanes = lambda h: slice(h * HEAD_DIM, (h + 1) * HEAD_DIM)

    def tile(i, q, h, j, carry, masked):
        start = pl.multiple_of(j * tk, tk)
        k = k_ref[pl.ds(start, tk), lanes(h)]
        v = v_ref[pl.ds(start, tk), lanes(h)]
        z = lax.dot_general(q, k, (((1,), (1,)), ((), ())), preferred_element_type=F32)
        e = jnp.exp(-jnp.abs(z))
        l1m = -(jnp.maximum(z, 0.0) + jnp.log(1.0 + e))
        if masked:
            t_idx = i * tq + lax.broadcasted_iota(jnp.int32, (tq, tk), 0)
            s_idx = j * tk + lax.broadcasted_iota(jnp.int32, (tq, tk), 1)
            causal = s_idx < t_idx
            l1m = jnp.where(causal, l1m, 0.0)
        hi = l1m.astype(BF16)
        lo = (l1m - hi.astype(F32)).astype(BF16)
        cs = jnp.dot(jnp.concatenate([hi, lo], axis=1), tri_ref[...], preferred_element_type=F32)
        incl = cs[:, :tk]
        tot = cs[:, tk:]
        w = jnp.exp(z + incl + carry)
        if masked:
            w = jnp.where(causal, w, 0.0)
        return jnp.dot(w.astype(BF16), v, preferred_element_type=F32), carry + tot

    def qtile(i, n_st):
        row0 = pl.multiple_of(i * tq, tq)
        q = [q_ref[pl.ds(row0, tq), lanes(h)] for h in heads]
        acc = [jnp.zeros((tq, HEAD_DIM), F32) for _ in heads]
        carry = [jnp.zeros((tq, tk), F32) for _ in heads]
        steps = [(i * n_diag + (n_diag - 1 - d), True) for d in range(n_diag)]
        steps += [(i * n_diag - 1 - s, False) for s in range(n_st)]
        for j, masked in steps:
            for h in heads:
                pv, carry[h] = tile(i, q[h], h, j, carry[h], masked)
                acc[h] = acc[h] + pv
        for h in heads:
            acc_ref[h] = acc[h]
            carry_ref[h] = carry[h]

        def cond(c):
            j, m = c
            return jnp.logical_and(j >= 0, m > SB_LOG_CUTOFF)

        def body(c):
            j, _ = c
            m = None
            for h in heads:
                pv, new_carry = tile(i, q[h], h, j, carry_ref[h], False)
                acc_ref[h] += pv
                carry_ref[h] = new_carry
                mh = jnp.max(new_carry)
                m = mh if m is None else jnp.maximum(m, mh)
            return j - 1, m

        m0 = functools.reduce(jnp.maximum, [jnp.max(c) for c in carry])
        lax.while_loop(cond, body, (i * n_diag - 1 - n_st, m0))
        for h in heads:
            o_ref[pl.ds(row0, tq), lanes(h)] = acc_ref[h].astype(o_ref.dtype)

    qtile(0, 0)
    if nq > 1:
        n_st = min(n_static, n_diag)

        def qbody(i, c):
            qtile(i, n_st)
            return c

        lax.fori_loop(1, nq, qbody, 0)


def _sb_prefix_matrix(tk):
    j = jnp.arange(tk)[:, None]
    s = jnp.arange(tk)[None, :]
    half = jnp.concatenate([(j >= s).astype(F32), jnp.ones((tk, tk), F32)], axis=1)
    return jnp.concatenate([half, half], axis=0).astype(BF16)


def _sb_attention(proj, bsz, seq):
    tq, tk = min(SB_TQ, seq), min(SB_TK, seq)
    tri = _sb_prefix_matrix(tk)
    hp = SB_HEADS_PER_STEP
    ng = SB_HEADS // hp
    col = lambda off: pl.BlockSpec((seq, hp * HEAD_DIM), lambda b, h: (b, off + h))
    return pl.pallas_call(
        functools.partial(_sb_kernel, tq=tq, tk=tk, n_static=SB_STATIC_TILES),
        grid=(bsz, ng),
        in_specs=[col(0), col(ng), col(2 * ng),
                  pl.BlockSpec((2 * tk, 2 * tk), lambda b, h: (0, 0))],
        out_specs=pl.BlockSpec((seq, hp * HEAD_DIM), lambda b, h: (b, h)),
        out_shape=jax.ShapeDtypeStruct((bsz * seq, SB_WIDTH), BF16),
        scratch_shapes=[pltpu.VMEM((hp, tq, HEAD_DIM), F32), pltpu.VMEM((hp, tq, tk), F32)],
        compiler_params=_cparams(("parallel", "arbitrary")),
        name="sb_attention",
    )(proj, proj, proj, tri)


def _gelu_tanh(x):
    c = math.sqrt(2.0 / math.pi)
    return 0.5 * x * (1.0 + jnp.tanh(c * (x + 0.044715 * (x * x * x))))


def _ssm_kernel(u_ref, bb_ref, cc_ref, a_ref, d_ref, o_ref, x_ref, s_ref, hp_ref, *, t_len, rb):
    n = u_ref.shape[2]
    sl = a_ref.shape[-1]
    for t in range(t_len):
        x_ref[t] = jnp.dot(u_ref[0, t], bb_ref[0], preferred_element_type=F32)

    def in_chunk(from_entering_state):
        def body(r, c):
            r0 = pl.multiple_of(r * rb, rb)
            rows = pl.ds(r0, rb)
            a_re = jnp.broadcast_to(a_ref[0, 0:1, :], (rb, sl))
            a_im = jnp.broadcast_to(a_ref[0, 1:2, :], (rb, sl))
            if from_entering_state:
                h_re, h_im = hp_ref[rows, 0:sl], hp_ref[rows, sl:2 * sl]
            else:
                h_re = h_im = jnp.zeros((rb, sl), F32)
            for t in range(t_len):
                x_re, x_im = x_ref[t, rows, 0:sl], x_ref[t, rows, sl:2 * sl]
                h_re, h_im = (a_re * h_re - a_im * h_im + x_re, a_re * h_im + a_im * h_re + x_im)
                if from_entering_state:
                    x_ref[t, rows, 0:sl] = h_re
                    x_ref[t, rows, sl:2 * sl] = h_im
            if not from_entering_state:
                s_ref[rows, 0:sl] = h_re
                s_ref[rows, sl:2 * sl] = h_im
            return c

        lax.fori_loop(0, n // rb, body, 0)

    in_chunk(False)

    at_re, at_im = a_ref[0, 2:3, :], a_ref[0, 3:4, :]

    def chunk_scan(i, h):
        h_re, h_im = h
        row = pl.ds(i, 1)
        hp_ref[row, 0:sl] = h_re
        hp_ref[row, sl:2 * sl] = h_im
        s_re, s_im = s_ref[row, 0:sl], s_ref[row, sl:2 * sl]
        return (at_re * h_re - at_im * h_im + s_re, at_re * h_im + at_im * h_re + s_im)

    zero = jnp.zeros((1, sl), F32)
    lax.fori_loop(0, n, chunk_scan, (zero, zero), unroll=8)

    in_chunk(True)

    for t in range(t_len):
        y = jnp.dot(x_ref[t].astype(BF16), cc_ref[0], preferred_element_type=F32)
        y = y + d_ref[0] * u_ref[0, t].astype(F32)
        o_ref[0, t] = _gelu_tanh(y).astype(o_ref.dtype)


def _ssm_tables(lam_re, lam_im, log_dt, b_re, b_im, c_re, c_im, d_skip):
    t_len = SSM_CHUNK
    depth, groups, pst = lam_re.shape
    gc, gp = SSM_GROUP, SSM_GROUPS_PER_STEP
    nb = groups // gp
    dt = jnp.exp(log_dt)[..., None]
    ar, ai = lam_re * dt, lam_im * dt
    a_re, a_im = jnp.exp(ar) * jnp.cos(ai), jnp.exp(ar) * jnp.sin(ai)
    at_re = jnp.exp(t_len * ar) * jnp.cos(t_len * ai)
    at_im = jnp.exp(t_len * ar) * jnp.sin(t_len * ai)
    den = lam_re * lam_re + lam_im * lam_im
    nr, ni = a_re - 1.0, a_im
    f_re = (nr * lam_re + ni * lam_im) / den
    f_im = (ni * lam_re - nr * lam_im) / den
    bb_re = f_re[..., None] * b_re - f_im[..., None] * b_im
    bb_im = f_re[..., None] * b_im + f_im[..., None] * b_re
    eye = jnp.eye(gp, dtype=F32)

    def in_block(m):
        m = m.reshape(depth, nb, gp, pst, gc).transpose(0, 1, 2, 4, 3)
        return (m[:, :, :, :, None, :] * eye[None, None, :, None, :, None]).reshape(depth, nb, gp * gc, gp * pst)

    def out_block(m):
        m = m.reshape(depth, nb, gp, gc, pst).transpose(0, 1, 2, 4, 3)
        return (m[:, :, :, :, None, :] * eye[None, None, :, None, :, None]).reshape(depth, nb, gp * pst, gp * gc)

    bb_t = jnp.concatenate([in_block(bb_re), in_block(bb_im)], axis=3).astype(BF16)
    cc_t = jnp.concatenate([out_block(c_re), out_block(-c_im)], axis=2).astype(BF16)
    lanes = lambda m: m.reshape(depth, nb, gp * pst)
    a_t = jnp.stack([lanes(a_re), lanes(a_im), lanes(at_re), lanes(at_im)], axis=2)
    d_t = d_skip.reshape(depth, nb, 1, gp * gc)
    return bb_t, cc_t, a_t, d_t


def _ssm_branch(u, tabs, bsz, seq):
    bb_t, cc_t, a_t, d_t = tabs
    nb, win, sl2 = bb_t.shape
    t_len = SSM_CHUNK
    n = seq // t_len
    rb = min(SSM_ROW_BLOCK, n)
    width = u.shape[1]
    ur = u.reshape(bsz, n, t_len, width).transpose(0, 2, 1, 3)
    blk = pl.BlockSpec((1, t_len, n, win), lambda b, k: (b, 0, 0, k))
    tab = lambda a: pl.BlockSpec((1,) + a.shape[1:], lambda b, k: (k, 0, 0))
    g = pl.pallas_call(
        functools.partial(_ssm_kernel, t_len=t_len, rb=rb),
        grid=(bsz, nb),
        in_specs=[blk, tab(bb_t), tab(cc_t), tab(a_t), tab(d_t)],
        out_specs=blk,
        out_shape=jax.ShapeDtypeStruct((bsz, t_len, n, width), BF16),
        scratch_shapes=[pltpu.VMEM((t_len, n, sl2), F32), pltpu.VMEM((n, sl2), F32),
                        pltpu.VMEM((n, sl2), F32)],
        compiler_params=_cparams(("parallel", "arbitrary")),
        name="ssm",
    )(ur, bb_t, cc_t, a_t, d_t)
    return g.transpose(0, 2, 1, 3).reshape(bsz * seq, width)


def _mem_kernel(q_ref, k_ref, v_ref, o_ref):
    for h in range(MEM_HEADS):
        sl = slice(h * HEAD_DIM, (h + 1) * HEAD_DIM)
        s = lax.dot_general(q_ref[:, sl], k_ref[:, sl], (((1,), (1,)), ((), ())),
                            preferred_element_type=F32)
        p = jnp.exp(s - jnp.max(s, axis=-1, keepdims=True))
        l = jnp.sum(p, axis=-1, keepdims=True)
        o = jnp.dot(p.astype(BF16), v_ref[:, sl], preferred_element_type=F32)
        o_ref[:, sl] = (o / l).astype(o_ref.dtype)


def _mem_attention(proj, kv, bsz, seq, n_mem, q_col_block):
    tq = min(512, seq)
    nq = seq // tq
    return pl.pallas_call(
        _mem_kernel,
        grid=(bsz, nq),
        in_specs=[
            pl.BlockSpec((tq, MEM_WIDTH), lambda b, i: (b * nq + i, q_col_block)),
            pl.BlockSpec((n_mem, MEM_WIDTH), lambda b, i: (b, 0)),
            pl.BlockSpec((n_mem, MEM_WIDTH), lambda b, i: (b, 1)),
        ],
        out_specs=pl.BlockSpec((tq, MEM_WIDTH), lambda b, i: (b * nq + i, 0)),
        out_shape=jax.ShapeDtypeStruct((bsz * seq, MEM_WIDTH), BF16),
        compiler_params=_cparams(("parallel", "arbitrary")),
        name="mem_attention",
    )(proj, kv, kv)


def _layer_norm(xf, g, b):
    mu = jnp.mean(xf, axis=-1, keepdims=True)
    xc = xf - mu
    var = jnp.mean(xc * xc, axis=-1, keepdims=True)
    return xc * lax.rsqrt(var + LN_EPS) * g + b


def _merge_kernel(sb_ref, g_ref, mm_ref, g0_ref, g1_ref, g2_ref, x_ref,
                  wsb_ref, wglu_ref, wsso_ref, wmem_ref, wo_ref, lng_ref, lnb_ref,
                  of_ref, ob_ref, *, alpha, ssm_width):
    p_sb = jnp.dot(sb_ref[...], wsb_ref[...], preferred_element_type=F32)
    glu = jnp.dot(g_ref[...], wglu_ref[...], preferred_element_type=F32)
    gated = glu[:, :ssm_width] * _sigmoid(glu[:, ssm_width:])
    p_ssm = jnp.dot(gated.astype(BF16), wsso_ref[...], preferred_element_type=F32)
    p_mem = jnp.dot(mm_ref[...], wmem_ref[...], preferred_element_type=F32)
    merged = (g0_ref[...].astype(F32) * p_sb + g1_ref[...].astype(F32) * p_ssm
              + g2_ref[...].astype(F32) * p_mem)
    mix = jnp.dot(merged.astype(BF16), wo_ref[...], preferred_element_type=F32)
    y = _layer_norm(alpha * x_ref[...] + mix, lng_ref[...], lnb_ref[...])
    of_ref[...] = y
    ob_ref[...] = y.astype(BF16)


def _const_spec(shape):
    return pl.BlockSpec(shape, lambda i: (0,) * len(shape), pipeline_mode=pl.Buffered(1))


def _merge(sb, g, mm, proj, x, wsb, wglu, wsso, wmem, wo, lng, lnb, *, alpha, gate_col_block, tm):
    m, d = x.shape
    ssm_width = g.shape[1]
    row = lambda w: pl.BlockSpec((tm, w), lambda i: (i, 0))
    gate = lambda k: pl.BlockSpec((tm, d), lambda i: (i, gate_col_block + k))
    return pl.pallas_call(
        functools.partial(_merge_kernel, alpha=alpha, ssm_width=ssm_width),
        grid=(m // tm,),
        in_specs=[row(sb.shape[1]), row(ssm_width), row(mm.shape[1]), gate(0), gate(1), gate(2), row(d),
                  _const_spec(wsb.shape), _const_spec(wglu.shape), _const_spec(wsso.shape),
                  _const_spec(wmem.shape), _const_spec(wo.shape), _const_spec(lng.shape),
                  _const_spec(lnb.shape)],
        out_specs=[row(d), row(d)],
        out_shape=[jax.ShapeDtypeStruct((m, d), F32), jax.ShapeDtypeStruct((m, d), BF16)],
        compiler_params=_cparams(("parallel",)),
        name="merge",
    )(sb, g, mm, proj, proj, proj, x, wsb, wglu, wsso, wmem, wo, lng, lnb)


def _ffn_kernel(xb_ref, x_ref, wg_ref, wu_ref, wd_ref, lng_ref, lnb_ref, of_ref, ob_ref, *, alpha):
    j = pl.program_id(1)

    @pl.when(j == 0)
    def _():
        of_ref[...] = jnp.zeros_like(of_ref)

    xb = xb_ref[...]
    fg = jnp.dot(xb, wg_ref[...], preferred_element_type=F32)
    fu = jnp.dot(xb, wu_ref[...], preferred_element_type=F32)
    h = (fg * _sigmoid(fg) * fu).astype(BF16)
    of_ref[...] += jnp.dot(h, wd_ref[...], preferred_element_type=F32)

    @pl.when(j == pl.num_programs(1) - 1)
    def _():
        y = _layer_norm(alpha * x_ref[...] + of_ref[...], lng_ref[...], lnb_ref[...])
        of_ref[...] = y
        ob_ref[...] = y.astype(BF16)


def _ffn(xb, x, wgu, wd, lng, lnb, *, alpha, tm, tf):
    m, d = x.shape
    d_ff = wd.shape[0]
    nf = d_ff // tf
    return pl.pallas_call(
        functools.partial(_ffn_kernel, alpha=alpha),
        grid=(m // tm, nf),
        in_specs=[
            pl.BlockSpec((tm, d), lambda i, j: (i, 0)),
            pl.BlockSpec((tm, d), lambda i, j: (i, 0)),
            pl.BlockSpec((d, tf), lambda i, j: (0, j)),
            pl.BlockSpec((d, tf), lambda i, j: (0, nf + j)),
            pl.BlockSpec((tf, d), lambda i, j: (j, 0)),
            pl.BlockSpec((1, d), lambda i, j: (0, 0)),
            pl.BlockSpec((1, d), lambda i, j: (0, 0)),
        ],
        out_specs=[pl.BlockSpec((tm, d), lambda i, j: (i, 0)), pl.BlockSpec((tm, d), lambda i, j: (i, 0))],
        out_shape=[jax.ShapeDtypeStruct((m, d), F32), jax.ShapeDtypeStruct((m, d), BF16)],
        compiler_params=_cparams(("parallel", "arbitrary")),
        name="ffn",
    )(xb, x, wgu, wgu, wd, lng, lnb)


def _forward(x, mem, w_in, b_in, sb_w_out, ssm_lambda_re, ssm_lambda_im, ssm_log_dt,
             ssm_b_re, ssm_b_im, ssm_c_re, ssm_c_im, ssm_d, ssm_w_glu, ssm_w_out,
             mem_w_kv, mem_w_out, w_o, ln1_g, ln1_b, ffn_w_gate_up, ffn_w_down,
             ln2_g, ln2_b, *, alpha):
    bsz, seq, d = x.shape
    depth = w_in.shape[0]
    n_mem = mem.shape[1]
    in_width = w_in.shape[2]
    ssm_width = ssm_d.shape[1]
    m = bsz * seq
    tn = 2048
    tm = min(1024, m)
    q_scale = HEAD_DIM ** -0.5

    u_off = 3 * SB_WIDTH
    qm_off = u_off + ssm_width
    gate_off = qm_off + MEM_WIDTH
    assert gate_off % tn == 0 and gate_off % d == 0 and qm_off % MEM_WIDTH == 0
    assert in_width == gate_off + N_BRANCHES * d
    col = jnp.arange(in_width)
    scale_vec = jnp.where((col < SB_WIDTH) | ((col >= qm_off) & (col < gate_off)), q_scale, 1.0)
    scale_vec = scale_vec.astype(F32)[None, :]

    tabs = _ssm_tables(ssm_lambda_re, ssm_lambda_im, ssm_log_dt, ssm_b_re, ssm_b_im,
                       ssm_c_re, ssm_c_im, ssm_d)
    bf = lambda a: a.astype(BF16)
    mem_b = bf(mem.reshape(bsz * n_mem, d))
    kv_width = mem_w_kv.shape[2]
    kv_zero = jnp.zeros((1, kv_width), F32)
    kv_one = jnp.ones((1, kv_width), F32)

    xf = x.reshape(m, d)
    xb = bf(xf)
    for l in range(depth):
        proj = _proj(xb, bf(w_in[l]), b_in[l][None, :], scale_vec, tm=tm, tn=tn, n_plain=gate_off // tn)
        sb = _sb_attention(proj, bsz, seq)
        g = _ssm_branch(proj[:, u_off:qm_off], tuple(t[l] for t in tabs), bsz, seq)
        kv = _proj(mem_b, bf(mem_w_kv[l]), kv_zero, kv_one, tm=min(1024, bsz * n_mem), tn=kv_width,
                   n_plain=1)
        mm = _mem_attention(proj, kv, bsz, seq, n_mem, qm_off // MEM_WIDTH)
        xf, xb = _merge(sb, g, mm, proj, xf, bf(sb_w_out[l]), bf(ssm_w_glu[l]), bf(ssm_w_out[l]),
                        bf(mem_w_out[l]), bf(w_o[l]), ln1_g[l][None, :], ln1_b[l][None, :], alpha=alpha,
                        gate_col_block=gate_off // d, tm=min(256, m))
        xf, xb = _ffn(xb, xf, bf(ffn_w_gate_up[l]), bf(ffn_w_down[l]), ln2_g[l][None, :], ln2_b[l][None, :],
                      alpha=alpha, tm=min(512, m), tf=512)
    return xf.reshape(bsz, seq, d)


def kernel(x, mem, w_in, b_in, sb_w_out, ssm_lambda_re, ssm_lambda_im, ssm_log_dt, ssm_b_re, ssm_b_im, ssm_c_re, ssm_c_im, ssm_d, ssm_w_glu, ssm_w_out, mem_w_kv, mem_w_out, w_o, ln1_g, ln1_b, ffn_w_gate_up, ffn_w_down, ln2_g, ln2_b):
    depth = w_in.shape[0]
    alpha = (2 * depth) ** 0.25
    return _forward(x, mem, w_in, b_in, sb_w_out, ssm_lambda_re, ssm_lambda_im, ssm_log_dt,
                    ssm_b_re, ssm_b_im, ssm_c_re, ssm_c_im, ssm_d, ssm_w_glu, ssm_w_out,
                    mem_w_kv, mem_w_out, w_o, ln1_g, ln1_b, ffn_w_gate_up, ffn_w_down,
                    ln2_g, ln2_b, alpha=alpha)
```

```python
import functools
import math

import jax
import jax.numpy as jnp
from jax import lax
from jax.experimental import pallas as pl
from jax.experimental.pallas import tpu as pltpu

F32 = jnp.float32
BF16 = jnp.bfloat16

SB_HEADS = 8
HEAD_DIM = 128
SB_WIDTH = SB_HEADS * HEAD_DIM
SSM_GROUP = 16
SSM_STATE = 64
MEM_HEADS = 4
MEM_WIDTH = MEM_HEADS * HEAD_DIM
N_BRANCHES = 3
LN_EPS = 1e-5

LANES = 128

SSM_CHUNK = 16
SSM_ROW_BLOCK = 16
SSM_GROUPS_PER_STEP = LANES // SSM_GROUP

SB_TQ = 256
SB_TK = 128
SB_LOG_CUTOFF = -110.0
SB_STATIC_TILES = 2
SB_HEADS_PER_STEP = 2

VMEM_LIMIT = 56 * 1024 * 1024


def _cparams(sem):
    return pltpu.CompilerParams(dimension_semantics=sem, vmem_limit_bytes=VMEM_LIMIT)


def _sigmoid(x):
    return 1.0 / (1.0 + jnp.exp(-x))


def _proj_kernel(x_ref, w_ref, b_ref, s_ref, o_ref, *, n_plain):
    acc = jnp.dot(x_ref[...], w_ref[...], preferred_element_type=F32)
    acc = (acc + b_ref[...]) * s_ref[...]
    is_gate = pl.program_id(1) >= n_plain
    o_ref[...] = jnp.where(is_gate, _sigmoid(acc), acc).astype(o_ref.dtype)


def _proj(x, w, b, s, layer, *, tm, tn, n_plain):
    m, k = x.shape
    n = w.shape[2]
    return pl.pallas_call(
        functools.partial(_proj_kernel, n_plain=n_plain),
        grid=(m // tm, n // tn),
        in_specs=[
            pl.BlockSpec((tm, k), lambda i, j: (i, 0)),
            pl.BlockSpec((None, k, tn), lambda i, j: (layer, 0, j)),
            pl.BlockSpec((None, 1, tn), lambda i, j: (layer, 0, j)),
            pl.BlockSpec((1, tn), lambda i, j: (0, j)),
        ],
        out_specs=pl.BlockSpec((tm, tn), lambda i, j: (i, j)),
        out_shape=jax.ShapeDtypeStruct((m, n), BF16),
        compiler_params=_cparams(("parallel", "arbitrary")),
        name="proj",
    )(x, w, b, s)


def _sb_kernel(q_ref, k_ref, v_ref, tri_ref, o_ref, acc_ref, carry_ref, *, tq, tk, n_static):
    nq = q_ref.shape[0] // tq
    n_diag = tq // tk
    heads = range(q_ref.shape[1] // HEAD_DIM)
    lanes = lambda h: slice(h * HEAD_DIM, (h + 1) * HEAD_DIM)

    def tile(i, q, h, j, carry, masked):
        start = pl.multiple_of(j * tk, tk)
        k = k_ref[pl.ds(start, tk), lanes(h)]
        v = v_ref[pl.ds(start, tk), lanes(h)]
        z = lax.dot_general(q, k, (((1,), (1,)), ((), ())), preferred_element_type=F32)
        e = jnp.exp(-jnp.abs(z))
        l1m = -(jnp.maximum(z, 0.0) + jnp.log(1.0 + e))
        if masked:
            t_idx = i * tq + lax.broadcasted_iota(jnp.int32, (tq, tk), 0)
            s_idx = j * tk + lax.broadcasted_iota(jnp.int32, (tq, tk), 1)
            causal = s_idx < t_idx
            l1m = jnp.where(causal, l1m, 0.0)
        hi = l1m.astype(BF16)
        lo = (l1m - hi.astype(F32)).astype(BF16)
        cs = jnp.dot(jnp.concatenate([hi, lo], axis=1), tri_ref[...], preferred_element_type=F32)
        incl = cs[:, :tk]
        tot = cs[:, tk:]
        w = jnp.exp(z + incl + carry)
        if masked:
            w = jnp.where(causal, w, 0.0)
        return jnp.dot(w.astype(BF16), v, preferred_element_type=F32), carry + tot

    def window(i, q, h, n_st):
        n_win = n_st + n_diag
        start = pl.multiple_of((i * n_diag - n_st) * tk, tk)
        kw = k_ref[pl.ds(start, n_win * tk), lanes(h)]
        vw = v_ref[pl.ds(start, n_win * tk), lanes(h)]
        z = lax.dot_general(q, kw, (((1,), (1,)), ((), ())), preferred_element_type=F32)
        row = lax.broadcasted_iota(jnp.int32, (tq, tk), 0)
        col = lax.broadcasted_iota(jnp.int32, (tq, tk), 1)
        zs, causal, hl = [], [], []
        for w in range(n_win):
            zw = z[:, w * tk:(w + 1) * tk]
            l1m = -(jnp.maximum(zw, 0.0) + jnp.log(1.0 + jnp.exp(-jnp.abs(zw))))
            d = w - n_st
            cw = (col + d * tk < row) if d >= 0 else None
            if cw is not None:
                l1m = jnp.where(cw, l1m, 0.0)
            hi = l1m.astype(BF16)
            lo = (l1m - hi.astype(F32)).astype(BF16)
            zs.append(zw)
            causal.append(cw)
            hl.append(jnp.concatenate([hi, lo], axis=1))
        cs = jnp.dot(jnp.concatenate(hl, axis=0), tri_ref[...], preferred_element_type=F32)
        carry = jnp.zeros((tq, tk), F32)
        ws = [None] * n_win
        for w in reversed(range(n_win)):
            incl = cs[w * tq:(w + 1) * tq, :tk]
            tot = cs[w * tq:(w + 1) * tq, tk:]
            ww = jnp.exp(zs[w] + incl + carry)
            if causal[w] is not None:
                ww = jnp.where(causal[w], ww, 0.0)
            ws[w] = ww.astype(BF16)
            carry = carry + tot
        acc = jnp.dot(jnp.concatenate(ws, axis=1), vw, preferred_element_type=F32)
        return acc, carry

    def qtile(i, n_st):
        row0 = pl.multiple_of(i * tq, tq)
        q = [q_ref[pl.ds(row0, tq), lanes(h)] for h in heads]
        carry = []
        for h in heads:
            acc_h, carry_h = window(i, q[h], h, n_st)
            acc_ref[h] = acc_h
            carry_ref[h] = carry_h
            carry.append(carry_h)

        def cond(c):
            j, m = c
            return jnp.logical_and(j >= 0, m > SB_LOG_CUTOFF)

        def body(c):
            j, _ = c
            m = None
            for h in heads:
                pv, new_carry = tile(i, q[h], h, j, carry_ref[h], False)
                acc_ref[h] += pv
                carry_ref[h] = new_carry
                mh = jnp.max(new_carry)
                m = mh if m is None else jnp.maximum(m, mh)
            return j - 1, m

        m0 = functools.reduce(jnp.maximum, [jnp.max(c) for c in carry])
        lax.while_loop(cond, body, (i * n_diag - 1 - n_st, m0))
        for h in heads:
            o_ref[pl.ds(row0, tq), lanes(h)] = acc_ref[h].astype(o_ref.dtype)

    qtile(0, 0)
    if nq > 1:
        n_st = min(n_static, n_diag)

        def qbody(i, c):
            qtile(i, n_st)
            return c

        lax.fori_loop(1, nq, qbody, 0)


def _sb_prefix_matrix(tk):
    j = jnp.arange(tk)[:, None]
    s = jnp.arange(tk)[None, :]
    half = jnp.concatenate([(j >= s).astype(F32), jnp.ones((tk, tk), F32)], axis=1)
    return jnp.concatenate([half, half], axis=0).astype(BF16)


def _sb_attention(proj, bsz, seq):
    tq, tk = min(SB_TQ, seq), min(SB_TK, seq)
    tri = _sb_prefix_matrix(tk)
    hp = SB_HEADS_PER_STEP
    ng = SB_HEADS // hp
    col = lambda off: pl.BlockSpec((seq, hp * HEAD_DIM), lambda b, h: (b, off + h))
    return pl.pallas_call(
        functools.partial(_sb_kernel, tq=tq, tk=tk, n_static=SB_STATIC_TILES),
        grid=(bsz, ng),
        in_specs=[col(0), col(ng), col(2 * ng),
                  pl.BlockSpec((2 * tk, 2 * tk), lambda b, h: (0, 0))],
        out_specs=pl.BlockSpec((seq, hp * HEAD_DIM), lambda b, h: (b, h)),
        out_shape=jax.ShapeDtypeStruct((bsz * seq, SB_WIDTH), BF16),
        scratch_shapes=[pltpu.VMEM((hp, tq, HEAD_DIM), F32), pltpu.VMEM((hp, tq, tk), F32)],
        compiler_params=_cparams(("parallel", "arbitrary")),
        name="sb_attention",
    )(proj, proj, proj, tri)


def _gelu_tanh(x):
    c = math.sqrt(2.0 / math.pi)
    return 0.5 * x * (1.0 + jnp.tanh(c * (x + 0.044715 * (x * x * x))))


def _ssm_kernel(u_ref, bb_ref, cc_ref, a_ref, d_ref, o_ref, x_ref, s_ref, hp_ref, *, t_len, rb):
    n = u_ref.shape[2]
    sl = a_ref.shape[-1]
    for t in range(t_len):
        x_ref[t] = jnp.dot(u_ref[0, t], bb_ref[0], preferred_element_type=F32)

    def in_chunk(from_entering_state):
        def body(r, c):
            r0 = pl.multiple_of(r * rb, rb)
            rows = pl.ds(r0, rb)
            a_re = jnp.broadcast_to(a_ref[0, 0:1, :], (rb, sl))
            a_im = jnp.broadcast_to(a_ref[0, 1:2, :], (rb, sl))
            if from_entering_state:
                h_re, h_im = hp_ref[rows, 0:sl], hp_ref[rows, sl:2 * sl]
            else:
                h_re = h_im = jnp.zeros((rb, sl), F32)
            for t in range(t_len):
                x_re, x_im = x_ref[t, rows, 0:sl], x_ref[t, rows, sl:2 * sl]
                h_re, h_im = (a_re * h_re - a_im * h_im + x_re, a_re * h_im + a_im * h_re + x_im)
                if from_entering_state:
                    x_ref[t, rows, 0:sl] = h_re
                    x_ref[t, rows, sl:2 * sl] = h_im
            if not from_entering_state:
                s_ref[rows, 0:sl] = h_re
                s_ref[rows, sl:2 * sl] = h_im
            return c

        lax.fori_loop(0, n // rb, body, 0)

    in_chunk(False)

    at_re, at_im = a_ref[0, 2:3, :], a_ref[0, 3:4, :]

    def chunk_scan(i, h):
        h_re, h_im = h
        row = pl.ds(i, 1)
        hp_ref[row, 0:sl] = h_re
        hp_ref[row, sl:2 * sl] = h_im
        s_re, s_im = s_ref[row, 0:sl], s_ref[row, sl:2 * sl]
        return (at_re * h_re - at_im * h_im + s_re, at_re * h_im + at_im * h_re + s_im)

    zero = jnp.zeros((1, sl), F32)
    lax.fori_loop(0, n, chunk_scan, (zero, zero), unroll=8)

    in_chunk(True)

    for t in range(t_len):
        y = jnp.dot(x_ref[t].astype(BF16), cc_ref[0], preferred_element_type=F32)
        y = y + d_ref[0] * u_ref[0, t].astype(F32)
        o_ref[0, t] = _gelu_tanh(y).astype(o_ref.dtype)


def _ssm_tables(lam_re, lam_im, log_dt, b_re, b_im, c_re, c_im, d_skip):
    t_len = SSM_CHUNK
    depth, groups, pst = lam_re.shape
    gc, gp = SSM_GROUP, SSM_GROUPS_PER_STEP
    nb = groups // gp
    dt = jnp.exp(log_dt)[..., None]
    ar, ai = lam_re * dt, lam_im * dt
    a_re, a_im = jnp.exp(ar) * jnp.cos(ai), jnp.exp(ar) * jnp.sin(ai)
    at_re = jnp.exp(t_len * ar) * jnp.cos(t_len * ai)
    at_im = jnp.exp(t_len * ar) * jnp.sin(t_len * ai)
    den = lam_re * lam_re + lam_im * lam_im
    nr, ni = a_re - 1.0, a_im
    f_re = (nr * lam_re + ni * lam_im) / den
    f_im = (ni * lam_re - nr * lam_im) / den
    bb_re = f_re[..., None] * b_re - f_im[..., None] * b_im
    bb_im = f_re[..., None] * b_im + f_im[..., None] * b_re
    eye = jnp.eye(gp, dtype=F32)

    def in_block(m):
        m = m.reshape(depth, nb, gp, pst, gc).transpose(0, 1, 2, 4, 3)
        return (m[:, :, :, :, None, :] * eye[None, None, :, None, :, None]).reshape(depth, nb, gp * gc, gp * pst)

    def out_block(m):
        m = m.reshape(depth, nb, gp, gc, pst).transpose(0, 1, 2, 4, 3)
        return (m[:, :, :, :, None, :] * eye[None, None, :, None, :, None]).reshape(depth, nb, gp * pst, gp * gc)

    bb_t = jnp.concatenate([in_block(bb_re), in_block(bb_im)], axis=3).astype(BF16)
    cc_t = jnp.concatenate([out_block(c_re), out_block(-c_im)], axis=2).astype(BF16)
    lanes = lambda m: m.reshape(depth, nb, gp * pst)
    a_t = jnp.stack([lanes(a_re), lanes(a_im), lanes(at_re), lanes(at_im)], axis=2)
    d_t = d_skip.reshape(depth, nb, 1, gp * gc)
    return bb_t, cc_t, a_t, d_t


def _ssm_branch(u, tabs, layer, bsz, seq):
    bb_t, cc_t, a_t, d_t = tabs
    _, nb, win, sl2 = bb_t.shape
    t_len = SSM_CHUNK
    n = seq // t_len
    rb = min(SSM_ROW_BLOCK, n)
    width = u.shape[1]
    ur = u.reshape(bsz, n, t_len, width).transpose(0, 2, 1, 3)
    blk = pl.BlockSpec((1, t_len, n, win), lambda b, k: (b, 0, 0, k))
    tab = lambda a: pl.BlockSpec((None, 1) + a.shape[2:], lambda b, k: (layer, k, 0, 0))
    g = pl.pallas_call(
        functools.partial(_ssm_kernel, t_len=t_len, rb=rb),
        grid=(bsz, nb),
        in_specs=[blk, tab(bb_t), tab(cc_t), tab(a_t), tab(d_t)],
        out_specs=blk,
        out_shape=jax.ShapeDtypeStruct((bsz, t_len, n, width), BF16),
        scratch_shapes=[pltpu.VMEM((t_len, n, sl2), F32), pltpu.VMEM((n, sl2), F32),
                        pltpu.VMEM((n, sl2), F32)],
        compiler_params=_cparams(("parallel", "arbitrary")),
        name="ssm",
    )(ur, bb_t, cc_t, a_t, d_t)
    return g.transpose(0, 2, 1, 3).reshape(bsz * seq, width)


def _mem_kernel(q_ref, k_ref, v_ref, o_ref):
    for h in range(MEM_HEADS):
        sl = slice(h * HEAD_DIM, (h + 1) * HEAD_DIM)
        s = lax.dot_general(q_ref[:, sl], k_ref[:, sl], (((1,), (1,)), ((), ())),
                            preferred_element_type=F32)
        p = jnp.exp(s - jnp.max(s, axis=-1, keepdims=True))
        l = jnp.sum(p, axis=-1, keepdims=True)
        o = jnp.dot(p.astype(BF16), v_ref[:, sl], preferred_element_type=F32)
        o_ref[:, sl] = (o / l).astype(o_ref.dtype)


def _mem_attention(proj, kv, bsz, seq, n_mem, q_col_block):
    tq = min(512, seq)
    nq = seq // tq
    return pl.pallas_call(
        _mem_kernel,
        grid=(bsz, nq),
        in_specs=[
            pl.BlockSpec((tq, MEM_WIDTH), lambda b, i: (b * nq + i, q_col_block)),
            pl.BlockSpec((n_mem, MEM_WIDTH), lambda b, i: (b, 0)),
            pl.BlockSpec((n_mem, MEM_WIDTH), lambda b, i: (b, 1)),
        ],
        out_specs=pl.BlockSpec((tq, MEM_WIDTH), lambda b, i: (b * nq + i, 0)),
        out_shape=jax.ShapeDtypeStruct((bsz * seq, MEM_WIDTH), BF16),
        compiler_params=_cparams(("parallel", "arbitrary")),
        name="mem_attention",
    )(proj, kv, kv)


def _layer_norm(xf, g, b):
    mu = jnp.mean(xf, axis=-1, keepdims=True)
    xc = xf - mu
    var = jnp.mean(xc * xc, axis=-1, keepdims=True)
    return xc * lax.rsqrt(var + LN_EPS) * g + b


def _merge_kernel(sb_ref, g_ref, mm_ref, g0_ref, g1_ref, g2_ref, x_ref,
                  wsb_ref, wglu_ref, wsso_ref, wmem_ref, wo_ref, lng_ref, lnb_ref,
                  of_ref, ob_ref, *, alpha, ssm_width):
    p_sb = jnp.dot(sb_ref[...], wsb_ref[...], preferred_element_type=F32)
    glu = jnp.dot(g_ref[...], wglu_ref[...], preferred_element_type=F32)
    gated = glu[:, :ssm_width] * _sigmoid(glu[:, ssm_width:])
    p_ssm = jnp.dot(gated.astype(BF16), wsso_ref[...], preferred_element_type=F32)
    p_mem = jnp.dot(mm_ref[...], wmem_ref[...], preferred_element_type=F32)
    merged = (g0_ref[...].astype(F32) * p_sb + g1_ref[...].astype(F32) * p_ssm
              + g2_ref[...].astype(F32) * p_mem)
    mix = jnp.dot(merged.astype(BF16), wo_ref[...], preferred_element_type=F32)
    y = _layer_norm(alpha * x_ref[...] + mix, lng_ref[...], lnb_ref[...])
    of_ref[...] = y
    ob_ref[...] = y.astype(BF16)


def _layer_spec(a, layer):
    return pl.BlockSpec((None,) + a.shape[1:], lambda i: (layer,) + (0,) * (a.ndim - 1),
                        pipeline_mode=pl.Buffered(1))


def _merge(sb, g, mm, proj, x, wsb, wglu, wsso, wmem, wo, lng, lnb, layer, *, alpha, gate_col_block, tm):
    m, d = x.shape
    ssm_width = g.shape[1]
    row = lambda w: pl.BlockSpec((tm, w), lambda i: (i, 0))
    gate = lambda k: pl.BlockSpec((tm, d), lambda i: (i, gate_col_block + k))
    return pl.pallas_call(
        functools.partial(_merge_kernel, alpha=alpha, ssm_width=ssm_width),
        grid=(m // tm,),
        in_specs=[row(sb.shape[1]), row(ssm_width), row(mm.shape[1]), gate(0), gate(1), gate(2), row(d),
                  *[_layer_spec(a, layer) for a in (wsb, wglu, wsso, wmem, wo, lng, lnb)]],
        out_specs=[row(d), row(d)],
        out_shape=[jax.ShapeDtypeStruct((m, d), F32), jax.ShapeDtypeStruct((m, d), BF16)],
        compiler_params=_cparams(("parallel",)),
        name="merge",
    )(sb, g, mm, proj, proj, proj, x, wsb, wglu, wsso, wmem, wo, lng, lnb)


def _ffn_kernel(xb_ref, x_ref, wg_ref, wu_ref, wd_ref, lng_ref, lnb_ref, of_ref, ob_ref, *, alpha):
    j = pl.program_id(1)

    @pl.when(j == 0)
    def _():
        of_ref[...] = jnp.zeros_like(of_ref)

    xb = xb_ref[...]
    fg = jnp.dot(xb, wg_ref[...], preferred_element_type=F32)
    fu = jnp.dot(xb, wu_ref[...], preferred_element_type=F32)
    h = (fg * _sigmoid(fg) * fu).astype(BF16)
    of_ref[...] += jnp.dot(h, wd_ref[...], preferred_element_type=F32)

    @pl.when(j == pl.num_programs(1) - 1)
    def _():
        y = _layer_norm(alpha * x_ref[...] + of_ref[...], lng_ref[...], lnb_ref[...])
        of_ref[...] = y
        ob_ref[...] = y.astype(BF16)


def _ffn(xb, x, wgu, wd, lng, lnb, layer, *, alpha, tm, tf):
    m, d = x.shape
    d_ff = wd.shape[1]
    nf = d_ff // tf
    return pl.pallas_call(
        functools.partial(_ffn_kernel, alpha=alpha),
        grid=(m // tm, nf),
        in_specs=[
            pl.BlockSpec((tm, d), lambda i, j: (i, 0)),
            pl.BlockSpec((tm, d), lambda i, j: (i, 0)),
            pl.BlockSpec((None, d, tf), lambda i, j: (layer, 0, j)),
            pl.BlockSpec((None, d, tf), lambda i, j: (layer, 0, nf + j)),
            pl.BlockSpec((None, tf, d), lambda i, j: (layer, j, 0)),
            pl.BlockSpec((None, 1, d), lambda i, j: (layer, 0, 0)),
            pl.BlockSpec((None, 1, d), lambda i, j: (layer, 0, 0)),
        ],
        out_specs=[pl.BlockSpec((tm, d), lambda i, j: (i, 0)), pl.BlockSpec((tm, d), lambda i, j: (i, 0))],
        out_shape=[jax.ShapeDtypeStruct((m, d), F32), jax.ShapeDtypeStruct((m, d), BF16)],
        compiler_params=_cparams(("parallel", "arbitrary")),
        name="ffn",
    )(xb, x, wgu, wgu, wd, lng, lnb)


def _forward(x, mem, w_in, b_in, sb_w_out, ssm_lambda_re, ssm_lambda_im, ssm_log_dt,
             ssm_b_re, ssm_b_im, ssm_c_re, ssm_c_im, ssm_d, ssm_w_glu, ssm_w_out,
             mem_w_kv, mem_w_out, w_o, ln1_g, ln1_b, ffn_w_gate_up, ffn_w_down,
             ln2_g, ln2_b, *, alpha):
    bsz, seq, d = x.shape
    depth = w_in.shape[0]
    n_mem = mem.shape[1]
    in_width = w_in.shape[2]
    ssm_width = ssm_d.shape[1]
    m = bsz * seq
    tn = 2048
    tm = min(1024, m)
    q_scale = HEAD_DIM ** -0.5

    u_off = 3 * SB_WIDTH
    qm_off = u_off + ssm_width
    gate_off = qm_off + MEM_WIDTH
    assert gate_off % tn == 0 and gate_off % d == 0 and qm_off % MEM_WIDTH == 0
    assert in_width == gate_off + N_BRANCHES * d
    col = jnp.arange(in_width)
    scale_vec = jnp.where((col < SB_WIDTH) | ((col >= qm_off) & (col < gate_off)), q_scale, 1.0)
    scale_vec = scale_vec.astype(F32)[None, :]

    tabs = _ssm_tables(ssm_lambda_re, ssm_lambda_im, ssm_log_dt, ssm_b_re, ssm_b_im,
                       ssm_c_re, ssm_c_im, ssm_d)
    bf = lambda a: a.astype(BF16)
    row3 = lambda a: a[:, None, :]
    w_in_b, wkv_b, wsb_b, wglu_b, wsso_b = bf(w_in), bf(mem_w_kv), bf(sb_w_out), bf(ssm_w_glu), bf(ssm_w_out)
    wmem_b, wo_b, wgu_b, wd_b = bf(mem_w_out), bf(w_o), bf(ffn_w_gate_up), bf(ffn_w_down)
    b_in3, ln1_g3, ln1_b3, ln2_g3, ln2_b3 = row3(b_in), row3(ln1_g), row3(ln1_b), row3(ln2_g), row3(ln2_b)
    mem_b = bf(mem.reshape(bsz * n_mem, d))
    kv_width = mem_w_kv.shape[2]
    kv_zero = jnp.zeros((depth, 1, kv_width), F32)
    kv_one = jnp.ones((1, kv_width), F32)

    xf = x.reshape(m, d)
    xb = bf(xf)
    for l in range(depth):
        proj = _proj(xb, w_in_b, b_in3, scale_vec, l, tm=tm, tn=tn, n_plain=gate_off // tn)
        sb = _sb_attention(proj, bsz, seq)
        g = _ssm_branch(proj[:, u_off:qm_off], tabs, l, bsz, seq)
        kv = _proj(mem_b, wkv_b, kv_zero, kv_one, l, tm=min(1024, bsz * n_mem), tn=kv_width, n_plain=1)
        mm = _mem_attention(proj, kv, bsz, seq, n_mem, qm_off // MEM_WIDTH)
        xf, xb = _merge(sb, g, mm, proj, xf, wsb_b, wglu_b, wsso_b, wmem_b, wo_b, ln1_g3, ln1_b3, l,
                        alpha=alpha, gate_col_block=gate_off // d, tm=min(256, m))
        xf, xb = _ffn(xb, xf, wgu_b, wd_b, ln2_g3, ln2_b3, l, alpha=alpha, tm=min(512, m), tf=512)
    return xf.reshape(bsz, seq, d)


def kernel(x, mem, w_in, b_in, sb_w_out, ssm_lambda_re, ssm_lambda_im, ssm_log_dt, ssm_b_re, ssm_b_im, ssm_c_re, ssm_c_im, ssm_d, ssm_w_glu, ssm_w_out, mem_w_kv, mem_w_out, w_o, ln1_g, ln1_b, ffn_w_gate_up, ffn_w_down, ln2_g, ln2_b):
    depth = w_in.shape[0]
    alpha = (2 * depth) ** 0.25
    return _forward(x, mem, w_in, b_in, sb_w_out, ssm_lambda_re, ssm_lambda_im, ssm_log_dt,
                    ssm_b_re, ssm_b_im, ssm_c_re, ssm_c_im, ssm_d, ssm_w_glu, ssm_w_out,
                    mem_w_kv, mem_w_out, w_o, ln1_g, ln1_b, ffn_w_gate_up, ffn_w_down,
                    ln2_g, ln2_b, alpha=alpha)
```

```python
import functools
import math

import jax
import jax.numpy as jnp
from jax import lax
from jax.experimental import pallas as pl
from jax.experimental.pallas import tpu as pltpu

F32 = jnp.float32
BF16 = jnp.bfloat16

SB_HEADS = 8
HEAD_DIM = 128
SB_WIDTH = SB_HEADS * HEAD_DIM
SSM_GROUP = 16
SSM_STATE = 64
MEM_HEADS = 4
MEM_WIDTH = MEM_HEADS * HEAD_DIM
N_BRANCHES = 3
LN_EPS = 1e-5

LANES = 128

SSM_CHUNK = 16
SSM_ROW_BLOCK = 16
SSM_GROUPS_PER_STEP = LANES // SSM_GROUP

SB_TQ = 256
SB_TK = 128
SB_LOG_CUTOFF = -110.0
SB_STATIC_TILES = 2
SB_HEADS_PER_STEP = 4

VMEM_LIMIT = 56 * 1024 * 1024


def _cparams(sem):
    return pltpu.CompilerParams(dimension_semantics=sem, vmem_limit_bytes=VMEM_LIMIT)


def _sigmoid(x):
    return 1.0 / (1.0 + jnp.exp(-x))


def _proj_kernel(x_ref, w_ref, b_ref, s_ref, o_ref, *, n_plain):
    acc = jnp.dot(x_ref[...], w_ref[...], preferred_element_type=F32)
    acc = (acc + b_ref[...]) * s_ref[...]
    is_gate = pl.program_id(1) >= n_plain
    o_ref[...] = jnp.where(is_gate, _sigmoid(acc), acc).astype(o_ref.dtype)


def _proj(x, w, b, s, layer, *, tm, tn, n_plain):
    m, k = x.shape
    n = w.shape[2]
    return pl.pallas_call(
        functools.partial(_proj_kernel, n_plain=n_plain),
        grid=(m // tm, n // tn),
        in_specs=[
            pl.BlockSpec((tm, k), lambda i, j: (i, 0)),
            pl.BlockSpec((None, k, tn), lambda i, j: (layer, 0, j)),
            pl.BlockSpec((None, 1, tn), lambda i, j: (layer, 0, j)),
            pl.BlockSpec((1, tn), lambda i, j: (0, j)),
        ],
        out_specs=pl.BlockSpec((tm, tn), lambda i, j: (i, j)),
        out_shape=jax.ShapeDtypeStruct((m, n), BF16),
        compiler_params=_cparams(("parallel", "arbitrary")),
        name="proj",
    )(x, w, b, s)


def _sb_kernel(q_ref, k_ref, v_ref, tri_ref, o_ref, acc_ref, carry_ref, z_ref, cs_ref, *, tq, tk, n_static):
    nq = q_ref.shape[0] // tq
    n_diag = tq // tk
    heads = range(q_ref.shape[1] // HEAD_DIM)
    lanes = lambda h: slice(h * HEAD_DIM, (h + 1) * HEAD_DIM)

    def tile(q, h, j, carry):
        start = pl.multiple_of(j * tk, tk)
        k = k_ref[pl.ds(start, tk), lanes(h)]
        v = v_ref[pl.ds(start, tk), lanes(h)]
        z = lax.dot_general(q, k, (((1,), (1,)), ((), ())), preferred_element_type=F32)
        sp = jnp.maximum(z, 0.0) + jnp.log(1.0 + jnp.exp(-jnp.abs(z)))
        cs = jnp.dot(sp.astype(BF16), tri_ref[...], preferred_element_type=F32)
        incl = cs[:, :tk]
        tot = cs[:, tk:]
        w = jnp.exp(z + incl + carry)
        return jnp.dot(w.astype(BF16), v, preferred_element_type=F32), carry + tot

    def causal_mask(w, n_st):
        d = w - n_st
        if d < 0:
            return None
        row = lax.broadcasted_iota(jnp.int32, (tq, tk), 0)
        col = lax.broadcasted_iota(jnp.int32, (tq, tk), 1)
        return col + d * tk < row

    def window_scores(i, h, n_st):
        n_win = n_st + n_diag
        q = q_ref[pl.ds(pl.multiple_of(i * tq, tq), tq), lanes(h)]
        start = pl.multiple_of((i * n_diag - n_st) * tk, tk)
        kw = k_ref[pl.ds(start, n_win * tk), lanes(h)]
        z = lax.dot_general(q, kw, (((1,), (1,)), ((), ())), preferred_element_type=F32)
        hl = []
        for w in range(n_win):
            zw = z[:, w * tk:(w + 1) * tk]
            sp = jnp.maximum(zw, 0.0) + jnp.log(1.0 + jnp.exp(-jnp.abs(zw)))
            cw = causal_mask(w, n_st)
            if cw is not None:
                sp = jnp.where(cw, sp, 0.0)
            hl.append(sp.astype(BF16))
        cs = jnp.dot(jnp.concatenate(hl, axis=0), tri_ref[...], preferred_element_type=F32)
        return z, cs

    def window_output(i, h, n_st, z, cs):
        n_win = n_st + n_diag
        start = pl.multiple_of((i * n_diag - n_st) * tk, tk)
        vw = v_ref[pl.ds(start, n_win * tk), lanes(h)]
        carry = jnp.zeros((tq, tk), F32)
        ws = [None] * n_win
        for w in reversed(range(n_win)):
            incl = cs[w * tq:(w + 1) * tq, :tk]
            tot = cs[w * tq:(w + 1) * tq, tk:]
            ww = jnp.exp(z[:, w * tk:(w + 1) * tk] + incl + carry)
            cw = causal_mask(w, n_st)
            if cw is not None:
                ww = jnp.where(cw, ww, 0.0)
            ws[w] = ww.astype(BF16)
            carry = carry + tot
        acc = jnp.dot(jnp.concatenate(ws, axis=1), vw, preferred_element_type=F32)
        return acc, carry

    def qtile(i, n_st, pipelined):
        row0 = pl.multiple_of(i * tq, tq)
        carry = []
        for h in heads:
            if pipelined:
                z, cs = z_ref[h], cs_ref[h]
            else:
                z, cs = window_scores(i, h, n_st)
            acc_h, carry_h = window_output(i, h, n_st, z, cs)
            acc_ref[h] = acc_h
            carry_ref[h] = carry_h
            carry.append(carry_h)
        if pipelined:
            nxt = jnp.minimum(i + 1, nq - 1)
            for h in heads:
                z_ref[h], cs_ref[h] = window_scores(nxt, h, n_st)

        def cond(c):
            j, m = c
            return jnp.logical_and(j >= 0, m > SB_LOG_CUTOFF)

        def body(c):
            j, _ = c
            m = None
            for h in heads:
                pv, new_carry = tile(q_ref[pl.ds(row0, tq), lanes(h)], h, j, carry_ref[h])
                acc_ref[h] += pv
                carry_ref[h] = new_carry
                mh = jnp.max(new_carry)
                m = mh if m is None else jnp.maximum(m, mh)
            return j - 1, m

        m0 = functools.reduce(jnp.maximum, [jnp.max(c) for c in carry])
        lax.while_loop(cond, body, (i * n_diag - 1 - n_st, m0))
        for h in heads:
            o_ref[pl.ds(row0, tq), lanes(h)] = acc_ref[h].astype(o_ref.dtype)

    qtile(0, 0, False)
    if nq > 1:
        n_st = n_static
        for h in heads:
            z_ref[h], cs_ref[h] = window_scores(1, h, n_st)

        def qbody(i, c):
            qtile(i, n_st, True)
            return c

        lax.fori_loop(1, nq, qbody, 0)


def _sb_prefix_matrix(tk):
    j = jnp.arange(tk)[:, None]
    s = jnp.arange(tk)[None, :]
    return -jnp.concatenate([(j >= s).astype(F32), jnp.ones((tk, tk), F32)], axis=1).astype(BF16)


def _sb_attention(proj, bsz, seq):
    tq, tk = min(SB_TQ, seq), min(SB_TK, seq)
    tri = _sb_prefix_matrix(tk)
    n_static = min(SB_STATIC_TILES, tq // tk)
    n_win = n_static + tq // tk
    hp = SB_HEADS_PER_STEP
    ng = SB_HEADS // hp
    col = lambda off: pl.BlockSpec((seq, hp * HEAD_DIM), lambda b, h: (b, off + h))
    return pl.pallas_call(
        functools.partial(_sb_kernel, tq=tq, tk=tk, n_static=n_static),
        grid=(bsz, ng),
        in_specs=[col(0), col(ng), col(2 * ng),
                  pl.BlockSpec((tk, 2 * tk), lambda b, h: (0, 0))],
        out_specs=pl.BlockSpec((seq, hp * HEAD_DIM), lambda b, h: (b, h)),
        out_shape=jax.ShapeDtypeStruct((bsz * seq, SB_WIDTH), BF16),
        scratch_shapes=[pltpu.VMEM((hp, tq, HEAD_DIM), F32), pltpu.VMEM((hp, tq, tk), F32),
                        pltpu.VMEM((hp, tq, n_win * tk), F32), pltpu.VMEM((hp, n_win * tq, 2 * tk), F32)],
        compiler_params=_cparams(("parallel", "arbitrary")),
        name="sb_attention",
    )(proj, proj, proj, tri)


def _gelu_tanh(x):
    c = math.sqrt(2.0 / math.pi)
    return 0.5 * x * (1.0 + jnp.tanh(c * (x + 0.044715 * (x * x * x))))


def _ssm_kernel(u_ref, bb_ref, cc_ref, a_ref, d_ref, o_ref, x_ref, s_ref, hp_ref, *, t_len, rb):
    n = u_ref.shape[2]
    sl = a_ref.shape[-1]
    for t in range(t_len):
        x_ref[t] = jnp.dot(u_ref[0, t], bb_ref[0], preferred_element_type=F32)

    def in_chunk(from_entering_state):
        def body(r, c):
            r0 = pl.multiple_of(r * rb, rb)
            rows = pl.ds(r0, rb)
            a_re = jnp.broadcast_to(a_ref[0, 0:1, :], (rb, sl))
            a_im = jnp.broadcast_to(a_ref[0, 1:2, :], (rb, sl))
            if from_entering_state:
                h_re, h_im = hp_ref[rows, 0:sl], hp_ref[rows, sl:2 * sl]
            else:
                h_re = h_im = jnp.zeros((rb, sl), F32)
            for t in range(t_len):
                x_re, x_im = x_ref[t, rows, 0:sl], x_ref[t, rows, sl:2 * sl]
                h_re, h_im = (a_re * h_re - a_im * h_im + x_re, a_re * h_im + a_im * h_re + x_im)
                if from_entering_state:
                    x_ref[t, rows, 0:sl] = h_re
                    x_ref[t, rows, sl:2 * sl] = h_im
            if not from_entering_state:
                s_ref[rows, 0:sl] = h_re
                s_ref[rows, sl:2 * sl] = h_im
            return c

        lax.fori_loop(0, n // rb, body, 0)

    in_chunk(False)

    at_re, at_im = a_ref[0, 2:3, :], a_ref[0, 3:4, :]

    def chunk_scan(i, h):
        h_re, h_im = h
        row = pl.ds(i, 1)
        hp_ref[row, 0:sl] = h_re
        hp_ref[row, sl:2 * sl] = h_im
        s_re, s_im = s_ref[row, 0:sl], s_ref[row, sl:2 * sl]
        return (at_re * h_re - at_im * h_im + s_re, at_re * h_im + at_im * h_re + s_im)

    zero = jnp.zeros((1, sl), F32)
    lax.fori_loop(0, n, chunk_scan, (zero, zero), unroll=8)

    in_chunk(True)

    for t in range(t_len):
        y = jnp.dot(x_ref[t].astype(BF16), cc_ref[0], preferred_element_type=F32)
        y = y + d_ref[0] * u_ref[0, t].astype(F32)
        o_ref[0, t] = _gelu_tanh(y).astype(o_ref.dtype)


def _ssm_tables(lam_re, lam_im, log_dt, b_re, b_im, c_re, c_im, d_skip):
    t_len = SSM_CHUNK
    depth, groups, pst = lam_re.shape
    gc, gp = SSM_GROUP, SSM_GROUPS_PER_STEP
    nb = groups // gp
    dt = jnp.exp(log_dt)[..., None]
    ar, ai = lam_re * dt, lam_im * dt
    a_re, a_im = jnp.exp(ar) * jnp.cos(ai), jnp.exp(ar) * jnp.sin(ai)
    at_re = jnp.exp(t_len * ar) * jnp.cos(t_len * ai)
    at_im = jnp.exp(t_len * ar) * jnp.sin(t_len * ai)
    den = lam_re * lam_re + lam_im * lam_im
    nr, ni = a_re - 1.0, a_im
    f_re = (nr * lam_re + ni * lam_im) / den
    f_im = (ni * lam_re - nr * lam_im) / den
    bb_re = f_re[..., None] * b_re - f_im[..., None] * b_im
    bb_im = f_re[..., None] * b_im + f_im[..., None] * b_re
    eye = jnp.eye(gp, dtype=F32)

    def in_block(m):
        m = m.reshape(depth, nb, gp, pst, gc).transpose(0, 1, 2, 4, 3)
        return (m[:, :, :, :, None, :] * eye[None, None, :, None, :, None]).reshape(depth, nb, gp * gc, gp * pst)

    def out_block(m):
        m = m.reshape(depth, nb, gp, gc, pst).transpose(0, 1, 2, 4, 3)
        return (m[:, :, :, :, None, :] * eye[None, None, :, None, :, None]).reshape(depth, nb, gp * pst, gp * gc)

    bb_t = jnp.concatenate([in_block(bb_re), in_block(bb_im)], axis=3).astype(BF16)
    cc_t = jnp.concatenate([out_block(c_re), out_block(-c_im)], axis=2).astype(BF16)
    lanes = lambda m: m.reshape(depth, nb, gp * pst)
    a_t = jnp.stack([lanes(a_re), lanes(a_im), lanes(at_re), lanes(at_im)], axis=2)
    d_t = d_skip.reshape(depth, nb, 1, gp * gc)
    return bb_t, cc_t, a_t, d_t


def _ssm_branch(u, tabs, layer, bsz, seq):
    bb_t, cc_t, a_t, d_t = tabs
    _, nb, win, sl2 = bb_t.shape
    t_len = SSM_CHUNK
    n = seq // t_len
    rb = min(SSM_ROW_BLOCK, n)
    width = u.shape[1]
    ur = u.reshape(bsz, n, t_len, width).transpose(0, 2, 1, 3)
    blk = pl.BlockSpec((1, t_len, n, win), lambda b, k: (b, 0, 0, k))
    tab = lambda a: pl.BlockSpec((None, 1) + a.shape[2:], lambda b, k: (layer, k, 0, 0))
    g = pl.pallas_call(
        functools.partial(_ssm_kernel, t_len=t_len, rb=rb),
        grid=(bsz, nb),
        in_specs=[blk, tab(bb_t), tab(cc_t), tab(a_t), tab(d_t)],
        out_specs=blk,
        out_shape=jax.ShapeDtypeStruct((bsz, t_len, n, width), BF16),
        scratch_shapes=[pltpu.VMEM((t_len, n, sl2), F32), pltpu.VMEM((n, sl2), F32),
                        pltpu.VMEM((n, sl2), F32)],
        compiler_params=_cparams(("parallel", "arbitrary")),
        name="ssm",
    )(ur, bb_t, cc_t, a_t, d_t)
    return g.transpose(0, 2, 1, 3).reshape(bsz * seq, width)


def _mem_kernel(q_ref, k_ref, v_ref, o_ref):
    for h in range(MEM_HEADS):
        sl = slice(h * HEAD_DIM, (h + 1) * HEAD_DIM)
        s = lax.dot_general(q_ref[:, sl], k_ref[:, sl], (((1,), (1,)), ((), ())),
                            preferred_element_type=F32)
        p = jnp.exp(s - jnp.max(s, axis=-1, keepdims=True))
        l = jnp.sum(p, axis=-1, keepdims=True)
        o = jnp.dot(p.astype(BF16), v_ref[:, sl], preferred_element_type=F32)
        o_ref[:, sl] = (o / l).astype(o_ref.dtype)


def _mem_attention(proj, kv, bsz, seq, n_mem, q_col_block):
    tq = min(512, seq)
    nq = seq // tq
    return pl.pallas_call(
        _mem_kernel,
        grid=(bsz, nq),
        in_specs=[
            pl.BlockSpec((tq, MEM_WIDTH), lambda b, i: (b * nq + i, q_col_block)),
            pl.BlockSpec((n_mem, MEM_WIDTH), lambda b, i: (b, 0)),
            pl.BlockSpec((n_mem, MEM_WIDTH), lambda b, i: (b, 1)),
        ],
        out_specs=pl.BlockSpec((tq, MEM_WIDTH), lambda b, i: (b * nq + i, 0)),
        out_shape=jax.ShapeDtypeStruct((bsz * seq, MEM_WIDTH), BF16),
        compiler_params=_cparams(("parallel", "arbitrary")),
        name="mem_attention",
    )(proj, kv, kv)


def _layer_norm(xf, g, b):
    mu = jnp.mean(xf, axis=-1, keepdims=True)
    xc = xf - mu
    var = jnp.mean(xc * xc, axis=-1, keepdims=True)
    return xc * lax.rsqrt(var + LN_EPS) * g + b


def _merge_kernel(sb_ref, g_ref, mm_ref, g0_ref, g1_ref, g2_ref, x_ref,
                  wsb_ref, wglu_ref, wsso_ref, wmem_ref, wo_ref, lng_ref, lnb_ref,
                  of_ref, ob_ref, *, alpha, ssm_width):
    p_sb = jnp.dot(sb_ref[...], wsb_ref[...], preferred_element_type=F32)
    glu = jnp.dot(g_ref[...], wglu_ref[...], preferred_element_type=F32)
    gated = glu[:, :ssm_width] * _sigmoid(glu[:, ssm_width:])
    p_ssm = jnp.dot(gated.astype(BF16), wsso_ref[...], preferred_element_type=F32)
    p_mem = jnp.dot(mm_ref[...], wmem_ref[...], preferred_element_type=F32)
    merged = (g0_ref[...].astype(F32) * p_sb + g1_ref[...].astype(F32) * p_ssm
              + g2_ref[...].astype(F32) * p_mem)
    mix = jnp.dot(merged.astype(BF16), wo_ref[...], preferred_element_type=F32)
    y = _layer_norm(alpha * x_ref[...] + mix, lng_ref[...], lnb_ref[...])
    of_ref[...] = y
    ob_ref[...] = y.astype(BF16)


def _layer_spec(a, layer):
    return pl.BlockSpec((None,) + a.shape[1:], lambda i: (layer,) + (0,) * (a.ndim - 1),
                        pipeline_mode=pl.Buffered(1))


def _merge(sb, g, mm, proj, x, wsb, wglu, wsso, wmem, wo, lng, lnb, layer, *, alpha, gate_col_block, tm):
    m, d = x.shape
    ssm_width = g.shape[1]
    row = lambda w: pl.BlockSpec((tm, w), lambda i: (i, 0))
    gate = lambda k: pl.BlockSpec((tm, d), lambda i: (i, gate_col_block + k))
    return pl.pallas_call(
        functools.partial(_merge_kernel, alpha=alpha, ssm_width=ssm_width),
        grid=(m // tm,),
        in_specs=[row(sb.shape[1]), row(ssm_width), row(mm.shape[1]), gate(0), gate(1), gate(2), row(d),
                  *[_layer_spec(a, layer) for a in (wsb, wglu, wsso, wmem, wo, lng, lnb)]],
        out_specs=[row(d), row(d)],
        out_shape=[jax.ShapeDtypeStruct((m, d), F32), jax.ShapeDtypeStruct((m, d), BF16)],
        compiler_params=_cparams(("parallel",)),
        name="merge",
    )(sb, g, mm, proj, proj, proj, x, wsb, wglu, wsso, wmem, wo, lng, lnb)


def _ffn_kernel(xb_ref, x_ref, wg_ref, wu_ref, wd_ref, lng_ref, lnb_ref, of_ref, ob_ref, *, alpha):
    j = pl.program_id(1)

    @pl.when(j == 0)
    def _():
        of_ref[...] = jnp.zeros_like(of_ref)

    xb = xb_ref[...]
    fg = jnp.dot(xb, wg_ref[...], preferred_element_type=F32)
    fu = jnp.dot(xb, wu_ref[...], preferred_element_type=F32)
    h = (fg * _sigmoid(fg) * fu).astype(BF16)
    of_ref[...] += jnp.dot(h, wd_ref[...], preferred_element_type=F32)

    @pl.when(j == pl.num_programs(1) - 1)
    def _():
        y = _layer_norm(alpha * x_ref[...] + of_ref[...], lng_ref[...], lnb_ref[...])
        of_ref[...] = y
        ob_ref[...] = y.astype(BF16)


def _ffn(xb, x, wgu, wd, lng, lnb, layer, *, alpha, tm, tf):
    m, d = x.shape
    d_ff = wd.shape[1]
    nf = d_ff // tf
    return pl.pallas_call(
        functools.partial(_ffn_kernel, alpha=alpha),
        grid=(m // tm, nf),
        in_specs=[
            pl.BlockSpec((tm, d), lambda i, j: (i, 0)),
            pl.BlockSpec((tm, d), lambda i, j: (i, 0)),
            pl.BlockSpec((None, d, tf), lambda i, j: (layer, 0, j)),
            pl.BlockSpec((None, d, tf), lambda i, j: (layer, 0, nf + j)),
            pl.BlockSpec((None, tf, d), lambda i, j: (layer, j, 0)),
            pl.BlockSpec((None, 1, d), lambda i, j: (layer, 0, 0)),
            pl.BlockSpec((None, 1, d), lambda i, j: (layer, 0, 0)),
        ],
        out_specs=[pl.BlockSpec((tm, d), lambda i, j: (i, 0)), pl.BlockSpec((tm, d), lambda i, j: (i, 0))],
        out_shape=[jax.ShapeDtypeStruct((m, d), F32), jax.ShapeDtypeStruct((m, d), BF16)],
        compiler_params=_cparams(("parallel", "arbitrary")),
        name="ffn",
    )(xb, x, wgu, wgu, wd, lng, lnb)


def _forward(x, mem, w_in, b_in, sb_w_out, ssm_lambda_re, ssm_lambda_im, ssm_log_dt,
             ssm_b_re, ssm_b_im, ssm_c_re, ssm_c_im, ssm_d, ssm_w_glu, ssm_w_out,
             mem_w_kv, mem_w_out, w_o, ln1_g, ln1_b, ffn_w_gate_up, ffn_w_down,
             ln2_g, ln2_b, *, alpha):
    bsz, seq, d = x.shape
    depth = w_in.shape[0]
    n_mem = mem.shape[1]
    in_width = w_in.shape[2]
    ssm_width = ssm_d.shape[1]
    m = bsz * seq
    tn = 2048
    tm = min(1024, m)
    q_scale = HEAD_DIM ** -0.5

    u_off = 3 * SB_WIDTH
    qm_off = u_off + ssm_width
    gate_off = qm_off + MEM_WIDTH
    assert gate_off % tn == 0 and gate_off % d == 0 and qm_off % MEM_WIDTH == 0
    assert in_width == gate_off + N_BRANCHES * d
    col = jnp.arange(in_width)
    scale_vec = jnp.where((col < SB_WIDTH) | ((col >= qm_off) & (col < gate_off)), q_scale, 1.0)
    scale_vec = scale_vec.astype(F32)[None, :]

    tabs = _ssm_tables(ssm_lambda_re, ssm_lambda_im, ssm_log_dt, ssm_b_re, ssm_b_im,
                       ssm_c_re, ssm_c_im, ssm_d)
    bf = lambda a: a.astype(BF16)
    row3 = lambda a: a[:, None, :]
    w_in_b, wkv_b, wsb_b, wglu_b, wsso_b = bf(w_in), bf(mem_w_kv), bf(sb_w_out), bf(ssm_w_glu), bf(ssm_w_out)
    wmem_b, wo_b, wgu_b, wd_b = bf(mem_w_out), bf(w_o), bf(ffn_w_gate_up), bf(ffn_w_down)
    b_in3, ln1_g3, ln1_b3, ln2_g3, ln2_b3 = row3(b_in), row3(ln1_g), row3(ln1_b), row3(ln2_g), row3(ln2_b)
    mem_b = bf(mem.reshape(bsz * n_mem, d))
    kv_width = mem_w_kv.shape[2]
    kv_zero = jnp.zeros((depth, 1, kv_width), F32)
    kv_one = jnp.ones((1, kv_width), F32)

    xf = x.reshape(m, d)
    xb = bf(xf)
    for l in range(depth):
        proj = _proj(xb, w_in_b, b_in3, scale_vec, l, tm=tm, tn=tn, n_plain=gate_off // tn)
        sb = _sb_attention(proj, bsz, seq)
        g = _ssm_branch(proj[:, u_off:qm_off], tabs, l, bsz, seq)
        kv = _proj(mem_b, wkv_b, kv_zero, kv_one, l, tm=min(1024, bsz * n_mem), tn=kv_width, n_plain=1)
        mm = _mem_attention(proj, kv, bsz, seq, n_mem, qm_off // MEM_WIDTH)
        xf, xb = _merge(sb, g, mm, proj, xf, wsb_b, wglu_b, wsso_b, wmem_b, wo_b, ln1_g3, ln1_b3, l,
                        alpha=alpha, gate_col_block=gate_off // d, tm=min(256, m))
        xf, xb = _ffn(xb, xf, wgu_b, wd_b, ln2_g3, ln2_b3, l, alpha=alpha, tm=min(512, m), tf=512)
    return xf.reshape(bsz, seq, d)


def kernel(x, mem, w_in, b_in, sb_w_out, ssm_lambda_re, ssm_lambda_im, ssm_log_dt, ssm_b_re, ssm_b_im, ssm_c_re, ssm_c_im, ssm_d, ssm_w_glu, ssm_w_out, mem_w_kv, mem_w_out, w_o, ln1_g, ln1_b, ffn_w_gate_up, ffn_w_down, ln2_g, ln2_b):
    depth = w_in.shape[0]
    alpha = (2 * depth) ** 0.25
    return _forward(x, mem, w_in, b_in, sb_w_out, ssm_lambda_re, ssm_lambda_im, ssm_log_dt,
                    ssm_b_re, ssm_b_im, ssm_c_re, ssm_c_im, ssm_d, ssm_w_glu, ssm_w_out,
                    mem_w_kv, mem_w_out, w_o, ln1_g, ln1_b, ffn_w_gate_up, ffn_w_down,
                    ln2_g, ln2_b, alpha=alpha)
```

```python
import functools
import math

import jax
import jax.numpy as jnp
from jax import lax
from jax.experimental import pallas as pl
from jax.experimental.pallas import tpu as pltpu

F32 = jnp.float32
BF16 = jnp.bfloat16

SB_HEADS = 8
HEAD_DIM = 128
SB_WIDTH = SB_HEADS * HEAD_DIM
SSM_GROUP = 16
SSM_STATE = 64
MEM_HEADS = 4
MEM_WIDTH = MEM_HEADS * HEAD_DIM
N_BRANCHES = 3
LN_EPS = 1e-5

LANES = 128

SSM_CHUNK = 16
SSM_ROW_BLOCK = 16
SSM_TIME_GROUP = 4
SSM_GROUPS_PER_STEP = LANES // SSM_GROUP

SB_TQ = 256
SB_TK = 128
SB_LOG_CUTOFF = -110.0
SB_STATIC_TILES = 2
SB_HEADS_PER_STEP = 4

VMEM_LIMIT = 56 * 1024 * 1024


def _cparams(sem):
    return pltpu.CompilerParams(dimension_semantics=sem, vmem_limit_bytes=VMEM_LIMIT)


def _sigmoid(x):
    return 1.0 / (1.0 + jnp.exp(-x))


def _proj_kernel(x_ref, w_ref, b_ref, s_ref, o_ref, *, n_plain):
    acc = jnp.dot(x_ref[...], w_ref[...], preferred_element_type=F32)
    acc = (acc + b_ref[...]) * s_ref[...]
    is_gate = pl.program_id(1) >= n_plain
    o_ref[...] = jnp.where(is_gate, _sigmoid(acc), acc).astype(o_ref.dtype)


def _proj(x, w, b, s, layer, *, tm, tn, n_plain):
    m, k = x.shape
    n = w.shape[2]
    return pl.pallas_call(
        functools.partial(_proj_kernel, n_plain=n_plain),
        grid=(m // tm, n // tn),
        in_specs=[
            pl.BlockSpec((tm, k), lambda i, j: (i, 0)),
            pl.BlockSpec((None, k, tn), lambda i, j: (layer, 0, j)),
            pl.BlockSpec((None, 1, tn), lambda i, j: (layer, 0, j)),
            pl.BlockSpec((1, tn), lambda i, j: (0, j)),
        ],
        out_specs=pl.BlockSpec((tm, tn), lambda i, j: (i, j)),
        out_shape=jax.ShapeDtypeStruct((m, n), BF16),
        compiler_params=_cparams(("parallel", "arbitrary")),
        name="proj",
    )(x, w, b, s)


def _sb_kernel(q_ref, k_ref, v_ref, tri_ref, o_ref, acc_ref, carry_ref, z_ref, cs_ref, *, tq, tk, n_static):
    nq = q_ref.shape[0] // tq
    n_diag = tq // tk
    heads = range(q_ref.shape[1] // HEAD_DIM)
    lanes = lambda h: slice(h * HEAD_DIM, (h + 1) * HEAD_DIM)

    def tile(q, h, j, carry):
        start = pl.multiple_of(j * tk, tk)
        k = k_ref[pl.ds(start, tk), lanes(h)]
        v = v_ref[pl.ds(start, tk), lanes(h)]
        z = lax.dot_general(q, k, (((1,), (1,)), ((), ())), preferred_element_type=F32)
        sp = jnp.maximum(z, 0.0) + jnp.log(1.0 + jnp.exp(-jnp.abs(z)))
        cs = jnp.dot(sp.astype(BF16), tri_ref[...], preferred_element_type=F32)
        incl = cs[:, :tk]
        tot = cs[:, tk:]
        w = jnp.exp(z + incl + carry)
        return jnp.dot(w.astype(BF16), v, preferred_element_type=F32), carry + tot

    def causal_mask(w, n_st):
        d = w - n_st
        if d < 0:
            return None
        row = lax.broadcasted_iota(jnp.int32, (tq, tk), 0)
        col = lax.broadcasted_iota(jnp.int32, (tq, tk), 1)
        return col + d * tk < row

    def window_scores(i, h, n_st):
        n_win = n_st + n_diag
        q = q_ref[pl.ds(pl.multiple_of(i * tq, tq), tq), lanes(h)]
        start = pl.multiple_of((i * n_diag - n_st) * tk, tk)
        kw = k_ref[pl.ds(start, n_win * tk), lanes(h)]
        z = lax.dot_general(q, kw, (((1,), (1,)), ((), ())), preferred_element_type=F32)
        hl = []
        for w in range(n_win):
            zw = z[:, w * tk:(w + 1) * tk]
            sp = jnp.maximum(zw, 0.0) + jnp.log(1.0 + jnp.exp(-jnp.abs(zw)))
            cw = causal_mask(w, n_st)
            if cw is not None:
                sp = jnp.where(cw, sp, 0.0)
            hl.append(sp.astype(BF16))
        cs = jnp.dot(jnp.concatenate(hl, axis=0), tri_ref[...], preferred_element_type=F32)
        return z, cs

    def window_output(i, h, n_st, z, cs):
        n_win = n_st + n_diag
        start = pl.multiple_of((i * n_diag - n_st) * tk, tk)
        vw = v_ref[pl.ds(start, n_win * tk), lanes(h)]
        carry = jnp.zeros((tq, tk), F32)
        ws = [None] * n_win
        for w in reversed(range(n_win)):
            incl = cs[w * tq:(w + 1) * tq, :tk]
            tot = cs[w * tq:(w + 1) * tq, tk:]
            ww = jnp.exp(z[:, w * tk:(w + 1) * tk] + incl + carry)
            cw = causal_mask(w, n_st)
            if cw is not None:
                ww = jnp.where(cw, ww, 0.0)
            ws[w] = ww.astype(BF16)
            carry = carry + tot
        acc = jnp.dot(jnp.concatenate(ws, axis=1), vw, preferred_element_type=F32)
        return acc, carry

    def qtile(i, n_st, pipelined):
        row0 = pl.multiple_of(i * tq, tq)
        carry = []
        for h in heads:
            if pipelined:
                z, cs = z_ref[h], cs_ref[h]
            else:
                z, cs = window_scores(i, h, n_st)
            acc_h, carry_h = window_output(i, h, n_st, z, cs)
            acc_ref[h] = acc_h
            carry_ref[h] = carry_h
            carry.append(carry_h)
        if pipelined:
            nxt = jnp.minimum(i + 1, nq - 1)
            for h in heads:
                z_ref[h], cs_ref[h] = window_scores(nxt, h, n_st)

        def cond(c):
            j, m = c
            return jnp.logical_and(j >= 0, m > SB_LOG_CUTOFF)

        def body(c):
            j, _ = c
            m = None
            for h in heads:
                pv, new_carry = tile(q_ref[pl.ds(row0, tq), lanes(h)], h, j, carry_ref[h])
                acc_ref[h] += pv
                carry_ref[h] = new_carry
                mh = jnp.max(new_carry)
                m = mh if m is None else jnp.maximum(m, mh)
            return j - 1, m

        m0 = functools.reduce(jnp.maximum, [jnp.max(c) for c in carry])
        lax.while_loop(cond, body, (i * n_diag - 1 - n_st, m0))
        for h in heads:
            o_ref[pl.ds(row0, tq), lanes(h)] = acc_ref[h].astype(o_ref.dtype)

    qtile(0, 0, False)
    if nq > 1:
        n_st = n_static
        for h in heads:
            z_ref[h], cs_ref[h] = window_scores(1, h, n_st)

        def qbody(i, c):
            qtile(i, n_st, True)
            return c

        lax.fori_loop(1, nq, qbody, 0)


def _sb_prefix_matrix(tk):
    j = jnp.arange(tk)[:, None]
    s = jnp.arange(tk)[None, :]
    return -jnp.concatenate([(j >= s).astype(F32), jnp.ones((tk, tk), F32)], axis=1).astype(BF16)


def _sb_attention(proj, bsz, seq):
    tq, tk = min(SB_TQ, seq), min(SB_TK, seq)
    tri = _sb_prefix_matrix(tk)
    n_static = min(SB_STATIC_TILES, tq // tk)
    n_win = n_static + tq // tk
    hp = SB_HEADS_PER_STEP
    ng = SB_HEADS // hp
    col = lambda off: pl.BlockSpec((seq, hp * HEAD_DIM), lambda b, h: (b, off + h))
    return pl.pallas_call(
        functools.partial(_sb_kernel, tq=tq, tk=tk, n_static=n_static),
        grid=(bsz, ng),
        in_specs=[col(0), col(ng), col(2 * ng),
                  pl.BlockSpec((tk, 2 * tk), lambda b, h: (0, 0))],
        out_specs=pl.BlockSpec((seq, hp * HEAD_DIM), lambda b, h: (b, h)),
        out_shape=jax.ShapeDtypeStruct((bsz * seq, SB_WIDTH), BF16),
        scratch_shapes=[pltpu.VMEM((hp, tq, HEAD_DIM), F32), pltpu.VMEM((hp, tq, tk), F32),
                        pltpu.VMEM((hp, tq, n_win * tk), F32), pltpu.VMEM((hp, n_win * tq, 2 * tk), F32)],
        compiler_params=_cparams(("parallel", "arbitrary")),
        name="sb_attention",
    )(proj, proj, proj, tri)


def _gelu_tanh(x):
    c = math.sqrt(2.0 / math.pi)
    return 0.5 * x * (1.0 + jnp.tanh(c * (x + 0.044715 * (x * x * x))))


def _ssm_kernel(u_ref, bb_ref, cc_ref, a_ref, d_ref, o_ref, x_ref, s_ref, hp_ref, *, t_len, rb):
    n = u_ref.shape[2]
    sl = a_ref.shape[-1]
    a_re = jnp.broadcast_to(a_ref[0, 0:1, :], (rb, sl))
    a_im = jnp.broadcast_to(a_ref[0, 1:2, :], (rb, sl))
    row_blocks = [slice(r * rb, (r + 1) * rb) for r in range(n // rb)]

    tg = SSM_TIME_GROUP

    def advance(t0, first, keep):
        for rows in row_blocks:
            if first and keep:
                h_re, h_im = hp_ref[rows, 0:sl], hp_ref[rows, sl:2 * sl]
            elif first:
                h_re = h_im = jnp.zeros((rb, sl), F32)
            elif keep:
                h_re, h_im = x_ref[t0 - 1, rows, 0:sl], x_ref[t0 - 1, rows, sl:2 * sl]
            else:
                h_re, h_im = s_ref[rows, 0:sl], s_ref[rows, sl:2 * sl]
            for t in range(t0, t0 + tg):
                x_re, x_im = x_ref[t, rows, 0:sl], x_ref[t, rows, sl:2 * sl]
                h_re, h_im = (a_re * h_re - a_im * h_im + x_re, a_re * h_im + a_im * h_re + x_im)
                if keep:
                    x_ref[t, rows, 0:sl] = h_re
                    x_ref[t, rows, sl:2 * sl] = h_im
            if not keep:
                s_ref[rows, 0:sl] = h_re
                s_ref[rows, sl:2 * sl] = h_im

    for t0 in range(0, t_len, tg):
        for t in range(t0, t0 + tg):
            x_ref[t] = jnp.dot(u_ref[0, t], bb_ref[0], preferred_element_type=F32)
        advance(t0, t0 == 0, False)

    at_re, at_im = a_ref[0, 2:3, :], a_ref[0, 3:4, :]

    def chunk_scan(i, h):
        h_re, h_im = h
        row = pl.ds(i, 1)
        hp_ref[row, 0:sl] = h_re
        hp_ref[row, sl:2 * sl] = h_im
        s_re, s_im = s_ref[row, 0:sl], s_ref[row, sl:2 * sl]
        return (at_re * h_re - at_im * h_im + s_re, at_re * h_im + at_im * h_re + s_im)

    zero = jnp.zeros((1, sl), F32)
    lax.fori_loop(0, n, chunk_scan, (zero, zero), unroll=8)

    for t0 in range(0, t_len, tg):
        advance(t0, t0 == 0, True)
        for t in range(t0, t0 + tg):
            y = jnp.dot(x_ref[t].astype(BF16), cc_ref[0], preferred_element_type=F32)
            y = y + d_ref[0] * u_ref[0, t].astype(F32)
            o_ref[0, t] = _gelu_tanh(y).astype(o_ref.dtype)


def _ssm_tables(lam_re, lam_im, log_dt, b_re, b_im, c_re, c_im, d_skip):
    t_len = SSM_CHUNK
    depth, groups, pst = lam_re.shape
    gc, gp = SSM_GROUP, SSM_GROUPS_PER_STEP
    nb = groups // gp
    dt = jnp.exp(log_dt)[..., None]
    ar, ai = lam_re * dt, lam_im * dt
    a_re, a_im = jnp.exp(ar) * jnp.cos(ai), jnp.exp(ar) * jnp.sin(ai)
    at_re = jnp.exp(t_len * ar) * jnp.cos(t_len * ai)
    at_im = jnp.exp(t_len * ar) * jnp.sin(t_len * ai)
    den = lam_re * lam_re + lam_im * lam_im
    nr, ni = a_re - 1.0, a_im
    f_re = (nr * lam_re + ni * lam_im) / den
    f_im = (ni * lam_re - nr * lam_im) / den
    bb_re = f_re[..., None] * b_re - f_im[..., None] * b_im
    bb_im = f_re[..., None] * b_im + f_im[..., None] * b_re
    eye = jnp.eye(gp, dtype=F32)

    def in_block(m):
        m = m.reshape(depth, nb, gp, pst, gc).transpose(0, 1, 2, 4, 3)
        return (m[:, :, :, :, None, :] * eye[None, None, :, None, :, None]).reshape(depth, nb, gp * gc, gp * pst)

    def out_block(m):
        m = m.reshape(depth, nb, gp, gc, pst).transpose(0, 1, 2, 4, 3)
        return (m[:, :, :, :, None, :] * eye[None, None, :, None, :, None]).reshape(depth, nb, gp * pst, gp * gc)

    bb_t = jnp.concatenate([in_block(bb_re), in_block(bb_im)], axis=3).astype(BF16)
    cc_t = jnp.concatenate([out_block(c_re), out_block(-c_im)], axis=2).astype(BF16)
    lanes = lambda m: m.reshape(depth, nb, gp * pst)
    a_t = jnp.stack([lanes(a_re), lanes(a_im), lanes(at_re), lanes(at_im)], axis=2)
    d_t = d_skip.reshape(depth, nb, 1, gp * gc)
    return bb_t, cc_t, a_t, d_t


def _ssm_branch(u, tabs, layer, bsz, seq):
    bb_t, cc_t, a_t, d_t = tabs
    _, nb, win, sl2 = bb_t.shape
    t_len = SSM_CHUNK
    n = seq // t_len
    rb = min(SSM_ROW_BLOCK, n)
    width = u.shape[1]
    ur = u.reshape(bsz, n, t_len, width).transpose(0, 2, 1, 3)
    blk = pl.BlockSpec((1, t_len, n, win), lambda b, k: (b, 0, 0, k))
    tab = lambda a: pl.BlockSpec((None, 1) + a.shape[2:], lambda b, k: (layer, k, 0, 0))
    g = pl.pallas_call(
        functools.partial(_ssm_kernel, t_len=t_len, rb=rb),
        grid=(bsz, nb),
        in_specs=[blk, tab(bb_t), tab(cc_t), tab(a_t), tab(d_t)],
        out_specs=blk,
        out_shape=jax.ShapeDtypeStruct((bsz, t_len, n, width), BF16),
        scratch_shapes=[pltpu.VMEM((t_len, n, sl2), F32), pltpu.VMEM((n, sl2), F32),
                        pltpu.VMEM((n, sl2), F32)],
        compiler_params=_cparams(("parallel", "arbitrary")),
        name="ssm",
    )(ur, bb_t, cc_t, a_t, d_t)
    return g.transpose(0, 2, 1, 3).reshape(bsz * seq, width)


def _mem_kernel(q_ref, k_ref, v_ref, o_ref):
    for h in range(MEM_HEADS):
        sl = slice(h * HEAD_DIM, (h + 1) * HEAD_DIM)
        s = lax.dot_general(q_ref[:, sl], k_ref[:, sl], (((1,), (1,)), ((), ())),
                            preferred_element_type=F32)
        p = jnp.exp(s - jnp.max(s, axis=-1, keepdims=True))
        l = jnp.sum(p, axis=-1, keepdims=True)
        o = jnp.dot(p.astype(BF16), v_ref[:, sl], preferred_element_type=F32)
        o_ref[:, sl] = (o / l).astype(o_ref.dtype)


def _mem_attention(proj, kv, bsz, seq, n_mem, q_col_block):
    tq = min(512, seq)
    nq = seq // tq
    return pl.pallas_call(
        _mem_kernel,
        grid=(bsz, nq),
        in_specs=[
            pl.BlockSpec((tq, MEM_WIDTH), lambda b, i: (b * nq + i, q_col_block)),
            pl.BlockSpec((n_mem, MEM_WIDTH), lambda b, i: (b, 0)),
            pl.BlockSpec((n_mem, MEM_WIDTH), lambda b, i: (b, 1)),
        ],
        out_specs=pl.BlockSpec((tq, MEM_WIDTH), lambda b, i: (b * nq + i, 0)),
        out_shape=jax.ShapeDtypeStruct((bsz * seq, MEM_WIDTH), BF16),
        compiler_params=_cparams(("parallel", "arbitrary")),
        name="mem_attention",
    )(proj, kv, kv)


def _layer_norm(xf, g, b):
    mu = jnp.mean(xf, axis=-1, keepdims=True)
    xc = xf - mu
    var = jnp.mean(xc * xc, axis=-1, keepdims=True)
    return xc * lax.rsqrt(var + LN_EPS) * g + b


def _merge_kernel(sb_ref, g_ref, mm_ref, g0_ref, g1_ref, g2_ref, x_ref,
                  wsb_ref, wglu_ref, wsso_ref, wmem_ref, wo_ref, lng_ref, lnb_ref,
                  of_ref, ob_ref, *, alpha, ssm_width):
    p_sb = jnp.dot(sb_ref[...], wsb_ref[...], preferred_element_type=F32)
    glu = jnp.dot(g_ref[...], wglu_ref[...], preferred_element_type=F32)
    gated = glu[:, :ssm_width] * _sigmoid(glu[:, ssm_width:])
    p_ssm = jnp.dot(gated.astype(BF16), wsso_ref[...], preferred_element_type=F32)
    p_mem = jnp.dot(mm_ref[...], wmem_ref[...], preferred_element_type=F32)
    merged = (g0_ref[...].astype(F32) * p_sb + g1_ref[...].astype(F32) * p_ssm
              + g2_ref[...].astype(F32) * p_mem)
    mix = jnp.dot(merged.astype(BF16), wo_ref[...], preferred_element_type=F32)
    y = _layer_norm(alpha * x_ref[...] + mix, lng_ref[...], lnb_ref[...])
    of_ref[...] = y
    ob_ref[...] = y.astype(BF16)


def _layer_spec(a, layer):
    return pl.BlockSpec((None,) + a.shape[1:], lambda i: (layer,) + (0,) * (a.ndim - 1),
                        pipeline_mode=pl.Buffered(1))


def _merge(sb, g, mm, proj, x, wsb, wglu, wsso, wmem, wo, lng, lnb, layer, *, alpha, gate_col_block, tm):
    m, d = x.shape
    ssm_width = g.shape[1]
    row = lambda w: pl.BlockSpec((tm, w), lambda i: (i, 0))
    gate = lambda k: pl.BlockSpec((tm, d), lambda i: (i, gate_col_block + k))
    return pl.pallas_call(
        functools.partial(_merge_kernel, alpha=alpha, ssm_width=ssm_width),
        grid=(m // tm,),
        in_specs=[row(sb.shape[1]), row(ssm_width), row(mm.shape[1]), gate(0), gate(1), gate(2), row(d),
                  *[_layer_spec(a, layer) for a in (wsb, wglu, wsso, wmem, wo, lng, lnb)]],
        out_specs=[row(d), row(d)],
        out_shape=[jax.ShapeDtypeStruct((m, d), F32), jax.ShapeDtypeStruct((m, d), BF16)],
        compiler_params=_cparams(("parallel",)),
        name="merge",
    )(sb, g, mm, proj, proj, proj, x, wsb, wglu, wsso, wmem, wo, lng, lnb)


def _ffn_kernel(xb_ref, x_ref, wg_ref, wu_ref, wd_ref, lng_ref, lnb_ref, of_ref, ob_ref, *, alpha):
    j = pl.program_id(1)

    @pl.when(j == 0)
    def _():
        of_ref[...] = alpha * x_ref[...]

    xb = xb_ref[...]
    fg = jnp.dot(xb, wg_ref[...], preferred_element_type=F32)
    fu = jnp.dot(xb, wu_ref[...], preferred_element_type=F32)
    h = (fg * _sigmoid(fg) * fu).astype(BF16)
    of_ref[...] += jnp.dot(h, wd_ref[...], preferred_element_type=F32)

    @pl.when(j == pl.num_programs(1) - 1)
    def _():
        y = _layer_norm(of_ref[...], lng_ref[...], lnb_ref[...])
        of_ref[...] = y
        ob_ref[...] = y.astype(BF16)


def _ffn(xb, x, wgu, wd, lng, lnb, layer, *, alpha, tm, tf):
    m, d = x.shape
    d_ff = wd.shape[1]
    nf = d_ff // tf
    return pl.pallas_call(
        functools.partial(_ffn_kernel, alpha=alpha),
        grid=(m // tm, nf),
        in_specs=[
            pl.BlockSpec((tm, d), lambda i, j: (i, 0)),
            pl.BlockSpec((tm, d), lambda i, j: (i, 0)),
            pl.BlockSpec((None, d, tf), lambda i, j: (layer, 0, j)),
            pl.BlockSpec((None, d, tf), lambda i, j: (layer, 0, nf + j)),
            pl.BlockSpec((None, tf, d), lambda i, j: (layer, j, 0)),
            pl.BlockSpec((None, 1, d), lambda i, j: (layer, 0, 0)),
            pl.BlockSpec((None, 1, d), lambda i, j: (layer, 0, 0)),
        ],
        out_specs=[pl.BlockSpec((tm, d), lambda i, j: (i, 0)), pl.BlockSpec((tm, d), lambda i, j: (i, 0))],
        out_shape=[jax.ShapeDtypeStruct((m, d), F32), jax.ShapeDtypeStruct((m, d), BF16)],
        compiler_params=_cparams(("parallel", "arbitrary")),
        name="ffn",
    )(xb, x, wgu, wgu, wd, lng, lnb)


def _forward(x, mem, w_in, b_in, sb_w_out, ssm_lambda_re, ssm_lambda_im, ssm_log_dt,
             ssm_b_re, ssm_b_im, ssm_c_re, ssm_c_im, ssm_d, ssm_w_glu, ssm_w_out,
             mem_w_kv, mem_w_out, w_o, ln1_g, ln1_b, ffn_w_gate_up, ffn_w_down,
             ln2_g, ln2_b, *, alpha):
    bsz, seq, d = x.shape
    depth = w_in.shape[0]
    n_mem = mem.shape[1]
    in_width = w_in.shape[2]
    ssm_width = ssm_d.shape[1]
    m = bsz * seq
    tn = 2048
    tm = min(1024, m)
    q_scale = HEAD_DIM ** -0.5

    u_off = 3 * SB_WIDTH
    qm_off = u_off + ssm_width
    gate_off = qm_off + MEM_WIDTH
    assert gate_off % tn == 0 and gate_off % d == 0 and qm_off % MEM_WIDTH == 0
    assert in_width == gate_off + N_BRANCHES * d
    col = jnp.arange(in_width)
    scale_vec = jnp.where((col < SB_WIDTH) | ((col >= qm_off) & (col < gate_off)), q_scale, 1.0)
    scale_vec = scale_vec.astype(F32)[None, :]

    tabs = _ssm_tables(ssm_lambda_re, ssm_lambda_im, ssm_log_dt, ssm_b_re, ssm_b_im,
                       ssm_c_re, ssm_c_im, ssm_d)
    bf = lambda a: a.astype(BF16)
    row3 = lambda a: a[:, None, :]
    w_in_b, wkv_b, wsb_b, wglu_b, wsso_b = bf(w_in), bf(mem_w_kv), bf(sb_w_out), bf(ssm_w_glu), bf(ssm_w_out)
    wmem_b, wo_b, wgu_b, wd_b = bf(mem_w_out), bf(w_o), bf(ffn_w_gate_up), bf(ffn_w_down)
    b_in3, ln1_g3, ln1_b3, ln2_g3, ln2_b3 = row3(b_in), row3(ln1_g), row3(ln1_b), row3(ln2_g), row3(ln2_b)
    mem_b = bf(mem.reshape(bsz * n_mem, d))
    kv_width = mem_w_kv.shape[2]
    kv_zero = jnp.zeros((depth, 1, kv_width), F32)
    kv_one = jnp.ones((1, kv_width), F32)

    xf = x.reshape(m, d)
    xb = bf(xf)
    for l in range(depth):
        proj = _proj(xb, w_in_b, b_in3, scale_vec, l, tm=tm, tn=tn, n_plain=gate_off // tn)
        sb = _sb_attention(proj, bsz, seq)
        g = _ssm_branch(proj[:, u_off:qm_off], tabs, l, bsz, seq)
        kv = _proj(mem_b, wkv_b, kv_zero, kv_one, l, tm=min(1024, bsz * n_mem), tn=kv_width, n_plain=1)
        mm = _mem_attention(proj, kv, bsz, seq, n_mem, qm_off // MEM_WIDTH)
        xf, xb = _merge(sb, g, mm, proj, xf, wsb_b, wglu_b, wsso_b, wmem_b, wo_b, ln1_g3, ln1_b3, l,
                        alpha=alpha, gate_col_block=gate_off // d, tm=min(256, m))
        xf, xb = _ffn(xb, xf, wgu_b, wd_b, ln2_g3, ln2_b3, l, alpha=alpha, tm=min(512, m), tf=512)
    return xf.reshape(bsz, seq, d)


def kernel(x, mem, w_in, b_in, sb_w_out, ssm_lambda_re, ssm_lambda_im, ssm_log_dt, ssm_b_re, ssm_b_im, ssm_c_re, ssm_c_im, ssm_d, ssm_w_glu, ssm_w_out, mem_w_kv, mem_w_out, w_o, ln1_g, ln1_b, ffn_w_gate_up, ffn_w_down, ln2_g, ln2_b):
    depth = w_in.shape[0]
    alpha = (2 * depth) ** 0.25
    return _forward(x, mem, w_in, b_in, sb_w_out, ssm_lambda_re, ssm_lambda_im, ssm_log_dt,
                    ssm_b_re, ssm_b_im, ssm_c_re, ssm_c_im, ssm_d, ssm_w_glu, ssm_w_out,
                    mem_w_kv, mem_w_out, w_o, ln1_g, ln1_b, ffn_w_gate_up, ffn_w_down,
                    ln2_g, ln2_b, alpha=alpha)
```

```python
import functools
import math

import jax
import jax.numpy as jnp
from jax import lax
from jax.experimental import pallas as pl
from jax.experimental.pallas import tpu as pltpu

F32 = jnp.float32
BF16 = jnp.bfloat16

SB_HEADS = 8
HEAD_DIM = 128
SB_WIDTH = SB_HEADS * HEAD_DIM
SSM_GROUP = 16
SSM_STATE = 64
MEM_HEADS = 4
MEM_WIDTH = MEM_HEADS * HEAD_DIM
N_BRANCHES = 3
LN_EPS = 1e-5

LANES = 128

SSM_CHUNK = 16
SSM_ROW_BLOCK = 16
SSM_TIME_GROUP = 4
SSM_GROUPS_PER_STEP = LANES // SSM_GROUP

SB_TQ = 256
SB_TK = 128
SB_LOG_CUTOFF = -110.0
SB_STATIC_TILES = 2
SB_HEADS_PER_STEP = 4

VMEM_LIMIT = 56 * 1024 * 1024


def _cparams(sem):
    return pltpu.CompilerParams(dimension_semantics=sem, vmem_limit_bytes=VMEM_LIMIT)


def _sigmoid(x):
    return 1.0 / (1.0 + jnp.exp(-x))


def _proj_kernel(x_ref, w_ref, b_ref, s_ref, o_ref, *, n_plain):
    acc = jnp.dot(x_ref[...], w_ref[...], preferred_element_type=F32)
    acc = (acc + b_ref[...]) * s_ref[...]
    is_gate = pl.program_id(1) >= n_plain
    o_ref[...] = jnp.where(is_gate, _sigmoid(acc), acc).astype(o_ref.dtype)


def _proj(x, w, b, s, layer, *, tm, tn, n_plain):
    m, k = x.shape
    n = w.shape[2]
    return pl.pallas_call(
        functools.partial(_proj_kernel, n_plain=n_plain),
        grid=(m // tm, n // tn),
        in_specs=[
            pl.BlockSpec((tm, k), lambda i, j: (i, 0)),
            pl.BlockSpec((None, k, tn), lambda i, j: (layer, 0, j)),
            pl.BlockSpec((None, 1, tn), lambda i, j: (layer, 0, j)),
            pl.BlockSpec((1, tn), lambda i, j: (0, j)),
        ],
        out_specs=pl.BlockSpec((tm, tn), lambda i, j: (i, j)),
        out_shape=jax.ShapeDtypeStruct((m, n), BF16),
        compiler_params=_cparams(("parallel", "arbitrary")),
        name="proj",
    )(x, w, b, s)


def _sb_kernel(q_ref, k_ref, v_ref, tri_ref, o_ref, acc_ref, carry_ref, z_ref, cs_ref, *, tq, tk, n_static):
    nq = q_ref.shape[0] // tq
    n_diag = tq // tk
    heads = range(q_ref.shape[1] // HEAD_DIM)
    lanes = lambda h: slice(h * HEAD_DIM, (h + 1) * HEAD_DIM)

    def tile(q, h, j, carry):
        start = pl.multiple_of(j * tk, tk)
        k = k_ref[pl.ds(start, tk), lanes(h)]
        v = v_ref[pl.ds(start, tk), lanes(h)]
        z = lax.dot_general(q, k, (((1,), (1,)), ((), ())), preferred_element_type=F32)
        sp = jnp.maximum(z, 0.0) + jnp.log(1.0 + jnp.exp(-jnp.abs(z)))
        cs = jnp.dot(sp.astype(BF16), tri_ref[...], preferred_element_type=F32)
        incl = cs[:, :tk]
        tot = cs[:, tk:]
        w = jnp.exp(z + incl + carry)
        return jnp.dot(w.astype(BF16), v, preferred_element_type=F32), carry + tot

    def causal_mask(w, n_st):
        d = w - n_st
        if d < 0:
            return None
        row = lax.broadcasted_iota(jnp.int32, (tq, tk), 0)
        col = lax.broadcasted_iota(jnp.int32, (tq, tk), 1)
        return col + d * tk < row

    def window_scores(i, h, n_st):
        n_win = n_st + n_diag
        q = q_ref[pl.ds(pl.multiple_of(i * tq, tq), tq), lanes(h)]
        start = pl.multiple_of((i * n_diag - n_st) * tk, tk)
        kw = k_ref[pl.ds(start, n_win * tk), lanes(h)]
        z = lax.dot_general(q, kw, (((1,), (1,)), ((), ())), preferred_element_type=F32)
        hl = []
        for w in range(n_win):
            zw = z[:, w * tk:(w + 1) * tk]
            sp = jnp.maximum(zw, 0.0) + jnp.log(1.0 + jnp.exp(-jnp.abs(zw)))
            cw = causal_mask(w, n_st)
            if cw is not None:
                sp = jnp.where(cw, sp, 0.0)
            hl.append(sp.astype(BF16))
        cs = jnp.dot(jnp.concatenate(hl, axis=0), tri_ref[...], preferred_element_type=F32)
        return z, cs

    def window_output(i, h, n_st, z, cs):
        n_win = n_st + n_diag
        start = pl.multiple_of((i * n_diag - n_st) * tk, tk)
        vw = v_ref[pl.ds(start, n_win * tk), lanes(h)]
        carry = jnp.zeros((tq, tk), F32)
        ws = [None] * n_win
        for w in reversed(range(n_win)):
            incl = cs[w * tq:(w + 1) * tq, :tk]
            tot = cs[w * tq:(w + 1) * tq, tk:]
            ww = jnp.exp(z[:, w * tk:(w + 1) * tk] + incl + carry)
            cw = causal_mask(w, n_st)
            if cw is not None:
                ww = jnp.where(cw, ww, 0.0)
            ws[w] = ww.astype(BF16)
            carry = carry + tot
        acc = jnp.dot(jnp.concatenate(ws, axis=1), vw, preferred_element_type=F32)
        return acc, carry

    def qtile(i, n_st, pipelined):
        row0 = pl.multiple_of(i * tq, tq)
        carry = []
        for h in heads:
            if pipelined:
                z, cs = z_ref[h], cs_ref[h]
            else:
                z, cs = window_scores(i, h, n_st)
            acc_h, carry_h = window_output(i, h, n_st, z, cs)
            acc_ref[h] = acc_h
            carry_ref[h] = carry_h
            carry.append(carry_h)
        if pipelined:
            nxt = jnp.minimum(i + 1, nq - 1)
            for h in heads:
                z_ref[h], cs_ref[h] = window_scores(nxt, h, n_st)

        def cond(c):
            j, m = c
            return jnp.logical_and(j >= 0, m > SB_LOG_CUTOFF)

        def body(c):
            j, _ = c
            m = None
            for h in heads:
                pv, new_carry = tile(q_ref[pl.ds(row0, tq), lanes(h)], h, j, carry_ref[h])
                acc_ref[h] += pv
                carry_ref[h] = new_carry
                mh = jnp.max(new_carry)
                m = mh if m is None else jnp.maximum(m, mh)
            return j - 1, m

        m0 = functools.reduce(jnp.maximum, [jnp.max(c) for c in carry])
        lax.while_loop(cond, body, (i * n_diag - 1 - n_st, m0))
        for h in heads:
            o_ref[pl.ds(row0, tq), lanes(h)] = acc_ref[h].astype(o_ref.dtype)

    qtile(0, 0, False)
    if nq > 1:
        n_st = n_static
        for h in heads:
            z_ref[h], cs_ref[h] = window_scores(1, h, n_st)

        def qbody(i, c):
            qtile(i, n_st, True)
            return c

        lax.fori_loop(1, nq, qbody, 0)


def _sb_prefix_matrix(tk):
    j = jnp.arange(tk)[:, None]
    s = jnp.arange(tk)[None, :]
    return -jnp.concatenate([(j >= s).astype(F32), jnp.ones((tk, tk), F32)], axis=1).astype(BF16)


def _sb_attention(proj, bsz, seq):
    tq, tk = min(SB_TQ, seq), min(SB_TK, seq)
    tri = _sb_prefix_matrix(tk)
    n_static = min(SB_STATIC_TILES, tq // tk)
    n_win = n_static + tq // tk
    hp = SB_HEADS_PER_STEP
    ng = SB_HEADS // hp
    col = lambda off: pl.BlockSpec((seq, hp * HEAD_DIM), lambda b, h: (b, off + h))
    return pl.pallas_call(
        functools.partial(_sb_kernel, tq=tq, tk=tk, n_static=n_static),
        grid=(bsz, ng),
        in_specs=[col(0), col(ng), col(2 * ng),
                  pl.BlockSpec((tk, 2 * tk), lambda b, h: (0, 0))],
        out_specs=pl.BlockSpec((seq, hp * HEAD_DIM), lambda b, h: (b, h)),
        out_shape=jax.ShapeDtypeStruct((bsz * seq, SB_WIDTH), BF16),
        scratch_shapes=[pltpu.VMEM((hp, tq, HEAD_DIM), F32), pltpu.VMEM((hp, tq, tk), F32),
                        pltpu.VMEM((hp, tq, n_win * tk), F32), pltpu.VMEM((hp, n_win * tq, 2 * tk), F32)],
        compiler_params=_cparams(("parallel", "arbitrary")),
        name="sb_attention",
    )(proj, proj, proj, tri)


def _gelu_tanh(x):
    c = math.sqrt(2.0 / math.pi)
    return 0.5 * x * (1.0 + jnp.tanh(c * (x + 0.044715 * (x * x * x))))


def _ssm_kernel(u_ref, bb_ref, cc_ref, a_ref, d_ref, o_ref, x_ref, s_ref, hp_ref, *, t_len, rb):
    n = u_ref.shape[2]
    sl = a_ref.shape[-1]
    a_re = jnp.broadcast_to(a_ref[0, 0:1, :], (rb, sl))
    a_im = jnp.broadcast_to(a_ref[0, 1:2, :], (rb, sl))
    row_blocks = [slice(r * rb, (r + 1) * rb) for r in range(n // rb)]

    tg = SSM_TIME_GROUP

    def advance(t0, first, keep):
        for rows in row_blocks:
            if first and keep:
                h_re, h_im = hp_ref[rows, 0:sl], hp_ref[rows, sl:2 * sl]
            elif first:
                h_re = h_im = jnp.zeros((rb, sl), F32)
            elif keep:
                h_re, h_im = x_ref[t0 - 1, rows, 0:sl], x_ref[t0 - 1, rows, sl:2 * sl]
            else:
                h_re, h_im = s_ref[rows, 0:sl], s_ref[rows, sl:2 * sl]
            for t in range(t0, t0 + tg):
                x_re, x_im = x_ref[t, rows, 0:sl], x_ref[t, rows, sl:2 * sl]
                h_re, h_im = (a_re * h_re - a_im * h_im + x_re, a_re * h_im + a_im * h_re + x_im)
                if keep:
                    x_ref[t, rows, 0:sl] = h_re
                    x_ref[t, rows, sl:2 * sl] = h_im
            if not keep:
                s_ref[rows, 0:sl] = h_re
                s_ref[rows, sl:2 * sl] = h_im

    for t0 in range(0, t_len, tg):
        for t in range(t0, t0 + tg):
            x_ref[t] = jnp.dot(u_ref[0, t], bb_ref[0], preferred_element_type=F32)
        advance(t0, t0 == 0, False)

    at_re, at_im = a_ref[0, 2:3, :], a_ref[0, 3:4, :]

    def chunk_scan(i, h):
        h_re, h_im = h
        row = pl.ds(i, 1)
        hp_ref[row, 0:sl] = h_re
        hp_ref[row, sl:2 * sl] = h_im
        s_re, s_im = s_ref[row, 0:sl], s_ref[row, sl:2 * sl]
        return (at_re * h_re - at_im * h_im + s_re, at_re * h_im + at_im * h_re + s_im)

    zero = jnp.zeros((1, sl), F32)
    lax.fori_loop(0, n, chunk_scan, (zero, zero), unroll=8)

    for t0 in range(0, t_len, tg):
        advance(t0, t0 == 0, True)
        for t in range(t0, t0 + tg):
            y = jnp.dot(x_ref[t].astype(BF16), cc_ref[0], preferred_element_type=F32)
            y = y + d_ref[0] * u_ref[0, t].astype(F32)
            o_ref[0, t] = _gelu_tanh(y).astype(o_ref.dtype)


def _ssm_tables(lam_re, lam_im, log_dt, b_re, b_im, c_re, c_im, d_skip):
    t_len = SSM_CHUNK
    depth, groups, pst = lam_re.shape
    gc, gp = SSM_GROUP, SSM_GROUPS_PER_STEP
    nb = groups // gp
    dt = jnp.exp(log_dt)[..., None]
    ar, ai = lam_re * dt, lam_im * dt
    a_re, a_im = jnp.exp(ar) * jnp.cos(ai), jnp.exp(ar) * jnp.sin(ai)
    at_re = jnp.exp(t_len * ar) * jnp.cos(t_len * ai)
    at_im = jnp.exp(t_len * ar) * jnp.sin(t_len * ai)
    den = lam_re * lam_re + lam_im * lam_im
    nr, ni = a_re - 1.0, a_im
    f_re = (nr * lam_re + ni * lam_im) / den
    f_im = (ni * lam_re - nr * lam_im) / den
    bb_re = f_re[..., None] * b_re - f_im[..., None] * b_im
    bb_im = f_re[..., None] * b_im + f_im[..., None] * b_re
    eye = jnp.eye(gp, dtype=F32)

    def in_block(m):
        m = m.reshape(depth, nb, gp, pst, gc).transpose(0, 1, 2, 4, 3)
        return (m[:, :, :, :, None, :] * eye[None, None, :, None, :, None]).reshape(depth, nb, gp * gc, gp * pst)

    def out_block(m):
        m = m.reshape(depth, nb, gp, gc, pst).transpose(0, 1, 2, 4, 3)
        return (m[:, :, :, :, None, :] * eye[None, None, :, None, :, None]).reshape(depth, nb, gp * pst, gp * gc)

    bb_t = jnp.concatenate([in_block(bb_re), in_block(bb_im)], axis=3).astype(BF16)
    cc_t = jnp.concatenate([out_block(c_re), out_block(-c_im)], axis=2).astype(BF16)
    lanes = lambda m: m.reshape(depth, nb, gp * pst)
    a_t = jnp.stack([lanes(a_re), lanes(a_im), lanes(at_re), lanes(at_im)], axis=2)
    d_t = d_skip.reshape(depth, nb, 1, gp * gc)
    return bb_t, cc_t, a_t, d_t


def _ssm_branch(u, tabs, layer, bsz, seq):
    bb_t, cc_t, a_t, d_t = tabs
    _, nb, win, sl2 = bb_t.shape
    t_len = SSM_CHUNK
    n = seq // t_len
    rb = min(SSM_ROW_BLOCK, n)
    width = u.shape[1]
    ur = u.reshape(bsz, n, t_len, width).transpose(0, 2, 1, 3)
    blk = pl.BlockSpec((1, t_len, n, win), lambda b, k: (b, 0, 0, k))
    tab = lambda a: pl.BlockSpec((None, 1) + a.shape[2:], lambda b, k: (layer, k, 0, 0))
    g = pl.pallas_call(
        functools.partial(_ssm_kernel, t_len=t_len, rb=rb),
        grid=(bsz, nb),
        in_specs=[blk, tab(bb_t), tab(cc_t), tab(a_t), tab(d_t)],
        out_specs=blk,
        out_shape=jax.ShapeDtypeStruct((bsz, t_len, n, width), BF16),
        scratch_shapes=[pltpu.VMEM((t_len, n, sl2), F32), pltpu.VMEM((n, sl2), F32),
                        pltpu.VMEM((n, sl2), F32)],
        compiler_params=_cparams(("parallel", "arbitrary")),
        name="ssm",
    )(ur, bb_t, cc_t, a_t, d_t)
    return g.transpose(0, 2, 1, 3).reshape(bsz * seq, width)


def _mem_kernel(q_ref, k_ref, v_ref, o_ref):
    for h in range(MEM_HEADS):
        sl = slice(h * HEAD_DIM, (h + 1) * HEAD_DIM)
        s = lax.dot_general(q_ref[:, sl], k_ref[:, sl], (((1,), (1,)), ((), ())),
                            preferred_element_type=F32)
        p = jnp.exp(s - jnp.max(s, axis=-1, keepdims=True))
        l = jnp.sum(p, axis=-1, keepdims=True)
        o = jnp.dot(p.astype(BF16), v_ref[:, sl], preferred_element_type=F32)
        o_ref[:, sl] = (o / l).astype(o_ref.dtype)


def _mem_attention(proj, kv, bsz, seq, n_mem, q_col_block):
    tq = min(1024, seq)
    nq = seq // tq
    return pl.pallas_call(
        _mem_kernel,
        grid=(bsz, nq),
        in_specs=[
            pl.BlockSpec((tq, MEM_WIDTH), lambda b, i: (b * nq + i, q_col_block)),
            pl.BlockSpec((n_mem, MEM_WIDTH), lambda b, i: (b, 0)),
            pl.BlockSpec((n_mem, MEM_WIDTH), lambda b, i: (b, 1)),
        ],
        out_specs=pl.BlockSpec((tq, MEM_WIDTH), lambda b, i: (b * nq + i, 0)),
        out_shape=jax.ShapeDtypeStruct((bsz * seq, MEM_WIDTH), BF16),
        compiler_params=_cparams(("parallel", "arbitrary")),
        name="mem_attention",
    )(proj, kv, kv)


def _layer_norm(xf, g, b):
    mu = jnp.mean(xf, axis=-1, keepdims=True)
    xc = xf - mu
    var = jnp.mean(xc * xc, axis=-1, keepdims=True)
    return xc * lax.rsqrt(var + LN_EPS) * g + b


def _merge_kernel(sb_ref, g_ref, mm_ref, g0_ref, g1_ref, g2_ref, x_ref,
                  wsb_ref, wglu_ref, wsso_ref, wmem_ref, wo_ref, lng_ref, lnb_ref,
                  of_ref, ob_ref, pend_ref, *, alpha, ssm_width):
    i = pl.program_id(0)

    @pl.when(i == 0)
    def _():
        pend_ref[...] = jnp.zeros_like(pend_ref)

    def step(prev_slot, slot):
        y = _layer_norm(pend_ref[prev_slot], lng_ref[...], lnb_ref[...])
        of_ref[...] = y
        ob_ref[...] = y.astype(BF16)
        p_sb = jnp.dot(sb_ref[...], wsb_ref[...], preferred_element_type=F32)
        glu = jnp.dot(g_ref[...], wglu_ref[...], preferred_element_type=F32)
        gated = glu[:, :ssm_width] * _sigmoid(glu[:, ssm_width:])
        p_ssm = jnp.dot(gated.astype(BF16), wsso_ref[...], preferred_element_type=F32)
        p_mem = jnp.dot(mm_ref[...], wmem_ref[...], preferred_element_type=F32)
        merged = (g0_ref[...].astype(F32) * p_sb + g1_ref[...].astype(F32) * p_ssm
                  + g2_ref[...].astype(F32) * p_mem)
        mix = jnp.dot(merged.astype(BF16), wo_ref[...], preferred_element_type=F32)
        pend_ref[slot] = alpha * x_ref[...] + mix

    pl.when(lax.rem(i, 2) == 0)(lambda: step(1, 0))
    pl.when(lax.rem(i, 2) == 1)(lambda: step(0, 1))


def _layer_spec(a, layer):
    return pl.BlockSpec((None,) + a.shape[1:], lambda i: (layer,) + (0,) * (a.ndim - 1),
                        pipeline_mode=pl.Buffered(1))


def _merge(sb, g, mm, proj, x, wsb, wglu, wsso, wmem, wo, lng, lnb, layer, *, alpha, gate_col_block, tm):
    m, d = x.shape
    ssm_width = g.shape[1]
    n = m // tm
    cur = lambda i: jnp.minimum(i, n - 1)
    prev = lambda i: jnp.maximum(i - 1, 0)
    row = lambda w: pl.BlockSpec((tm, w), lambda i: (cur(i), 0))
    gate = lambda k: pl.BlockSpec((tm, d), lambda i: (cur(i), gate_col_block + k))
    out = pl.BlockSpec((tm, d), lambda i: (prev(i), 0))
    return pl.pallas_call(
        functools.partial(_merge_kernel, alpha=alpha, ssm_width=ssm_width),
        grid=(n + 1,),
        in_specs=[row(sb.shape[1]), row(ssm_width), row(mm.shape[1]), gate(0), gate(1), gate(2), row(d),
                  *[_layer_spec(a, layer) for a in (wsb, wglu, wsso, wmem, wo, lng, lnb)]],
        out_specs=[out, out],
        out_shape=[jax.ShapeDtypeStruct((m, d), F32), jax.ShapeDtypeStruct((m, d), BF16)],
        scratch_shapes=[pltpu.VMEM((2, tm, d), F32)],
        compiler_params=_cparams(("arbitrary",)),
        name="merge",
    )(sb, g, mm, proj, proj, proj, x, wsb, wglu, wsso, wmem, wo, lng, lnb)


def _ffn_kernel(xb_ref, x_ref, wg_ref, wu_ref, wd_ref, lng_ref, lnb_ref, of_ref, ob_ref, *, alpha):
    j = pl.program_id(1)

    @pl.when(j == 0)
    def _():
        of_ref[...] = alpha * x_ref[...]

    xb = xb_ref[...]
    fg = jnp.dot(xb, wg_ref[...], preferred_element_type=F32)
    fu = jnp.dot(xb, wu_ref[...], preferred_element_type=F32)
    h = (fg * _sigmoid(fg) * fu).astype(BF16)
    of_ref[...] += jnp.dot(h, wd_ref[...], preferred_element_type=F32)

    @pl.when(j == pl.num_programs(1) - 1)
    def _():
        y = _layer_norm(of_ref[...], lng_ref[...], lnb_ref[...])
        of_ref[...] = y
        ob_ref[...] = y.astype(BF16)


def _ffn(xb, x, wgu, wd, lng, lnb, layer, *, alpha, tm, tf):
    m, d = x.shape
    d_ff = wd.shape[1]
    nf = d_ff // tf
    return pl.pallas_call(
        functools.partial(_ffn_kernel, alpha=alpha),
        grid=(m // tm, nf),
        in_specs=[
            pl.BlockSpec((tm, d), lambda i, j: (i, 0)),
            pl.BlockSpec((tm, d), lambda i, j: (i, 0)),
            pl.BlockSpec((None, d, tf), lambda i, j: (layer, 0, j)),
            pl.BlockSpec((None, d, tf), lambda i, j: (layer, 0, nf + j)),
            pl.BlockSpec((None, tf, d), lambda i, j: (layer, j, 0)),
            pl.BlockSpec((None, 1, d), lambda i, j: (layer, 0, 0)),
            pl.BlockSpec((None, 1, d), lambda i, j: (layer, 0, 0)),
        ],
        out_specs=[pl.BlockSpec((tm, d), lambda i, j: (i, 0)), pl.BlockSpec((tm, d), lambda i, j: (i, 0))],
        out_shape=[jax.ShapeDtypeStruct((m, d), F32), jax.ShapeDtypeStruct((m, d), BF16)],
        compiler_params=_cparams(("parallel", "arbitrary")),
        name="ffn",
    )(xb, x, wgu, wgu, wd, lng, lnb)


def _forward(x, mem, w_in, b_in, sb_w_out, ssm_lambda_re, ssm_lambda_im, ssm_log_dt,
             ssm_b_re, ssm_b_im, ssm_c_re, ssm_c_im, ssm_d, ssm_w_glu, ssm_w_out,
             mem_w_kv, mem_w_out, w_o, ln1_g, ln1_b, ffn_w_gate_up, ffn_w_down,
             ln2_g, ln2_b, *, alpha):
    bsz, seq, d = x.shape
    depth = w_in.shape[0]
    n_mem = mem.shape[1]
    in_width = w_in.shape[2]
    ssm_width = ssm_d.shape[1]
    m = bsz * seq
    tn = 2048
    tm = min(1024, m)
    q_scale = HEAD_DIM ** -0.5

    u_off = 3 * SB_WIDTH
    qm_off = u_off + ssm_width
    gate_off = qm_off + MEM_WIDTH
    assert gate_off % tn == 0 and gate_off % d == 0 and qm_off % MEM_WIDTH == 0
    assert in_width == gate_off + N_BRANCHES * d
    col = jnp.arange(in_width)
    scale_vec = jnp.where((col < SB_WIDTH) | ((col >= qm_off) & (col < gate_off)), q_scale, 1.0)
    scale_vec = scale_vec.astype(F32)[None, :]

    tabs = _ssm_tables(ssm_lambda_re, ssm_lambda_im, ssm_log_dt, ssm_b_re, ssm_b_im,
                       ssm_c_re, ssm_c_im, ssm_d)
    bf = lambda a: a.astype(BF16)
    row3 = lambda a: a[:, None, :]
    w_in_b, wkv_b, wsb_b, wglu_b, wsso_b = bf(w_in), bf(mem_w_kv), bf(sb_w_out), bf(ssm_w_glu), bf(ssm_w_out)
    wmem_b, wo_b, wgu_b, wd_b = bf(mem_w_out), bf(w_o), bf(ffn_w_gate_up), bf(ffn_w_down)
    b_in3, ln1_g3, ln1_b3, ln2_g3, ln2_b3 = row3(b_in), row3(ln1_g), row3(ln1_b), row3(ln2_g), row3(ln2_b)
    mem_b = bf(mem.reshape(bsz * n_mem, d))
    kv_width = mem_w_kv.shape[2]
    kv_zero = jnp.zeros((depth, 1, kv_width), F32)
    kv_one = jnp.ones((1, kv_width), F32)

    xf = x.reshape(m, d)
    xb = bf(xf)
    for l in range(depth):
        proj = _proj(xb, w_in_b, b_in3, scale_vec, l, tm=tm, tn=tn, n_plain=gate_off // tn)
        sb = _sb_attention(proj, bsz, seq)
        g = _ssm_branch(proj[:, u_off:qm_off], tabs, l, bsz, seq)
        kv = _proj(mem_b, wkv_b, kv_zero, kv_one, l, tm=min(1024, bsz * n_mem), tn=kv_width, n_plain=1)
        mm = _mem_attention(proj, kv, bsz, seq, n_mem, qm_off // MEM_WIDTH)
        xf, xb = _merge(sb, g, mm, proj, xf, wsb_b, wglu_b, wsso_b, wmem_b, wo_b, ln1_g3, ln1_b3, l,
                        alpha=alpha, gate_col_block=gate_off // d, tm=min(256, m))
        xf, xb = _ffn(xb, xf, wgu_b, wd_b, ln2_g3, ln2_b3, l, alpha=alpha, tm=min(512, m), tf=512)
    return xf.reshape(bsz, seq, d)


def kernel(x, mem, w_in, b_in, sb_w_out, ssm_lambda_re, ssm_lambda_im, ssm_log_dt, ssm_b_re, ssm_b_im, ssm_c_re, ssm_c_im, ssm_d, ssm_w_glu, ssm_w_out, mem_w_kv, mem_w_out, w_o, ln1_g, ln1_b, ffn_w_gate_up, ffn_w_down, ln2_g, ln2_b):
    depth = w_in.shape[0]
    alpha = (2 * depth) ** 0.25
    return _forward(x, mem, w_in, b_in, sb_w_out, ssm_lambda_re, ssm_lambda_im, ssm_log_dt,
                    ssm_b_re, ssm_b_im, ssm_c_re, ssm_c_im, ssm_d, ssm_w_glu, ssm_w_out,
                    mem_w_kv, mem_w_out, w_o, ln1_g, ln1_b, ffn_w_gate_up, ffn_w_down,
                    ln2_g, ln2_b, alpha=alpha)
```

```python
import functools
import math

import jax
import jax.numpy as jnp
from jax import lax
from jax.experimental import pallas as pl
from jax.experimental.pallas import tpu as pltpu

F32 = jnp.float32
BF16 = jnp.bfloat16

SB_HEADS = 8
HEAD_DIM = 128
SB_WIDTH = SB_HEADS * HEAD_DIM
SSM_GROUP = 16
SSM_STATE = 64
MEM_HEADS = 4
MEM_WIDTH = MEM_HEADS * HEAD_DIM
N_BRANCHES = 3
LN_EPS = 1e-5

LANES = 128

SSM_CHUNK = 16
SSM_ROW_BLOCK = 16
SSM_TIME_GROUP = 4
SSM_GROUPS_PER_STEP = LANES // SSM_GROUP

SB_TQ = 256
SB_TK = 128
SB_LOG_CUTOFF = -110.0
SB_STATIC_TILES = 2
SB_HEADS_PER_STEP = 4

VMEM_LIMIT = 56 * 1024 * 1024


def _cparams(sem):
    return pltpu.CompilerParams(dimension_semantics=sem, vmem_limit_bytes=VMEM_LIMIT)


def _sigmoid(x):
    return 1.0 / (1.0 + jnp.exp(-x))


def _proj_kernel(x_ref, w_ref, b_ref, s_ref, o_ref, *, n_plain):
    acc = jnp.dot(x_ref[...].astype(BF16), w_ref[...], preferred_element_type=F32)
    acc = (acc + b_ref[...]) * s_ref[...]
    is_gate = pl.program_id(1) >= n_plain
    o_ref[...] = jnp.where(is_gate, _sigmoid(acc), acc).astype(o_ref.dtype)


def _proj(x, w, b, s, layer, *, tm, tn, n_plain):
    m, k = x.shape
    n = w.shape[2]
    return pl.pallas_call(
        functools.partial(_proj_kernel, n_plain=n_plain),
        grid=(m // tm, n // tn),
        in_specs=[
            pl.BlockSpec((tm, k), lambda i, j: (i, 0)),
            pl.BlockSpec((None, k, tn), lambda i, j: (layer, 0, j)),
            pl.BlockSpec((None, 1, tn), lambda i, j: (layer, 0, j)),
            pl.BlockSpec((1, tn), lambda i, j: (0, j)),
        ],
        out_specs=pl.BlockSpec((tm, tn), lambda i, j: (i, j)),
        out_shape=jax.ShapeDtypeStruct((m, n), BF16),
        compiler_params=_cparams(("parallel", "arbitrary")),
        name="proj",
    )(x, w, b, s)


def _sb_kernel(q_ref, k_ref, v_ref, tri_ref, o_ref, acc_ref, carry_ref, z_ref, cs_ref, *, tq, tk, n_static):
    nq = q_ref.shape[0] // tq
    n_diag = tq // tk
    heads = range(q_ref.shape[1] // HEAD_DIM)
    lanes = lambda h: slice(h * HEAD_DIM, (h + 1) * HEAD_DIM)

    def tile(q, h, j, carry):
        start = pl.multiple_of(j * tk, tk)
        k = k_ref[pl.ds(start, tk), lanes(h)]
        v = v_ref[pl.ds(start, tk), lanes(h)]
        z = lax.dot_general(q, k, (((1,), (1,)), ((), ())), preferred_element_type=F32)
        sp = jnp.maximum(z, 0.0) + jnp.log(1.0 + jnp.exp(-jnp.abs(z)))
        cs = jnp.dot(sp.astype(BF16), tri_ref[...], preferred_element_type=F32)
        incl = cs[:, :tk]
        tot = cs[:, tk:]
        w = jnp.exp(z + incl + carry)
        return jnp.dot(w.astype(BF16), v, preferred_element_type=F32), carry + tot

    def causal_mask(w, n_st):
        d = w - n_st
        if d < 0:
            return None
        row = lax.broadcasted_iota(jnp.int32, (tq, tk), 0)
        col = lax.broadcasted_iota(jnp.int32, (tq, tk), 1)
        return col + d * tk < row

    def window_scores(i, h, n_st):
        n_win = n_st + n_diag
        q = q_ref[pl.ds(pl.multiple_of(i * tq, tq), tq), lanes(h)]
        start = pl.multiple_of((i * n_diag - n_st) * tk, tk)
        kw = k_ref[pl.ds(start, n_win * tk), lanes(h)]
        z = lax.dot_general(q, kw, (((1,), (1,)), ((), ())), preferred_element_type=F32)
        hl = []
        for w in range(n_win):
            zw = z[:, w * tk:(w + 1) * tk]
            sp = jnp.maximum(zw, 0.0) + jnp.log(1.0 + jnp.exp(-jnp.abs(zw)))
            cw = causal_mask(w, n_st)
            if cw is not None:
                sp = jnp.where(cw, sp, 0.0)
            hl.append(sp.astype(BF16))
        cs = jnp.dot(jnp.concatenate(hl, axis=0), tri_ref[...], preferred_element_type=F32)
        return z, cs

    def window_output(i, h, n_st, z, cs):
        n_win = n_st + n_diag
        start = pl.multiple_of((i * n_diag - n_st) * tk, tk)
        vw = v_ref[pl.ds(start, n_win * tk), lanes(h)]
        carry = jnp.zeros((tq, tk), F32)
        ws = [None] * n_win
        for w in reversed(range(n_win)):
            incl = cs[w * tq:(w + 1) * tq, :tk]
            tot = cs[w * tq:(w + 1) * tq, tk:]
            ww = jnp.exp(z[:, w * tk:(w + 1) * tk] + incl + carry)
            cw = causal_mask(w, n_st)
            if cw is not None:
                ww = jnp.where(cw, ww, 0.0)
            ws[w] = ww.astype(BF16)
            carry = carry + tot
        acc = jnp.dot(jnp.concatenate(ws, axis=1), vw, preferred_element_type=F32)
        return acc, carry

    def qtile(i, n_st, pipelined):
        row0 = pl.multiple_of(i * tq, tq)
        carry = []
        for h in heads:
            if pipelined:
                z, cs = z_ref[h], cs_ref[h]
            else:
                z, cs = window_scores(i, h, n_st)
            acc_h, carry_h = window_output(i, h, n_st, z, cs)
            acc_ref[h] = acc_h
            carry_ref[h] = carry_h
            carry.append(carry_h)
        if pipelined:
            nxt = jnp.minimum(i + 1, nq - 1)
            for h in heads:
                z_ref[h], cs_ref[h] = window_scores(nxt, h, n_st)

        def cond(c):
            j, m = c
            return jnp.logical_and(j >= 0, m > SB_LOG_CUTOFF)

        def body(c):
            j, _ = c
            m = None
            for h in heads:
                pv, new_carry = tile(q_ref[pl.ds(row0, tq), lanes(h)], h, j, carry_ref[h])
                acc_ref[h] += pv
                carry_ref[h] = new_carry
                mh = jnp.max(new_carry)
                m = mh if m is None else jnp.maximum(m, mh)
            return j - 1, m

        m0 = functools.reduce(jnp.maximum, [jnp.max(c) for c in carry])
        lax.while_loop(cond, body, (i * n_diag - 1 - n_st, m0))
        for h in heads:
            o_ref[pl.ds(row0, tq), lanes(h)] = acc_ref[h].astype(o_ref.dtype)

    qtile(0, 0, False)
    if nq > 1:
        n_st = n_static
        for h in heads:
            z_ref[h], cs_ref[h] = window_scores(1, h, n_st)

        def qbody(i, c):
            qtile(i, n_st, True)
            return c

        lax.fori_loop(1, nq, qbody, 0)


def _sb_prefix_matrix(tk):
    j = jnp.arange(tk)[:, None]
    s = jnp.arange(tk)[None, :]
    return -jnp.concatenate([(j >= s).astype(F32), jnp.ones((tk, tk), F32)], axis=1).astype(BF16)


def _sb_attention(proj, bsz, seq):
    tq, tk = min(SB_TQ, seq), min(SB_TK, seq)
    tri = _sb_prefix_matrix(tk)
    n_static = min(SB_STATIC_TILES, tq // tk)
    n_win = n_static + tq // tk
    hp = SB_HEADS_PER_STEP
    ng = SB_HEADS // hp
    col = lambda off: pl.BlockSpec((seq, hp * HEAD_DIM), lambda b, h: (b, off + h))
    return pl.pallas_call(
        functools.partial(_sb_kernel, tq=tq, tk=tk, n_static=n_static),
        grid=(bsz, ng),
        in_specs=[col(0), col(ng), col(2 * ng),
                  pl.BlockSpec((tk, 2 * tk), lambda b, h: (0, 0))],
        out_specs=pl.BlockSpec((seq, hp * HEAD_DIM), lambda b, h: (b, h)),
        out_shape=jax.ShapeDtypeStruct((bsz * seq, SB_WIDTH), BF16),
        scratch_shapes=[pltpu.VMEM((hp, tq, HEAD_DIM), F32), pltpu.VMEM((hp, tq, tk), F32),
                        pltpu.VMEM((hp, tq, n_win * tk), F32), pltpu.VMEM((hp, n_win * tq, 2 * tk), F32)],
        compiler_params=_cparams(("parallel", "arbitrary")),
        name="sb_attention",
    )(proj, proj, proj, tri)


def _gelu_tanh(x):
    c = math.sqrt(2.0 / math.pi)
    return 0.5 * x * (1.0 + jnp.tanh(c * (x + 0.044715 * (x * x * x))))


def _ssm_kernel(u_ref, bb_ref, cc_ref, a_ref, d_ref, o_ref, x_ref, s_ref, hp_ref, *, t_len, rb):
    n = u_ref.shape[2]
    sl = a_ref.shape[-1]
    a_re = jnp.broadcast_to(a_ref[0, 0:1, :], (rb, sl))
    a_im = jnp.broadcast_to(a_ref[0, 1:2, :], (rb, sl))
    row_blocks = [slice(r * rb, (r + 1) * rb) for r in range(n // rb)]

    tg = SSM_TIME_GROUP

    def advance(t0, first, keep):
        for rows in row_blocks:
            if first and keep:
                h_re, h_im = hp_ref[rows, 0:sl], hp_ref[rows, sl:2 * sl]
            elif first:
                h_re = h_im = jnp.zeros((rb, sl), F32)
            elif keep:
                h_re, h_im = x_ref[t0 - 1, rows, 0:sl], x_ref[t0 - 1, rows, sl:2 * sl]
            else:
                h_re, h_im = s_ref[rows, 0:sl], s_ref[rows, sl:2 * sl]
            for t in range(t0, t0 + tg):
                x_re, x_im = x_ref[t, rows, 0:sl], x_ref[t, rows, sl:2 * sl]
                h_re, h_im = (a_re * h_re - a_im * h_im + x_re, a_re * h_im + a_im * h_re + x_im)
                if keep:
                    x_ref[t, rows, 0:sl] = h_re
                    x_ref[t, rows, sl:2 * sl] = h_im
            if not keep:
                s_ref[rows, 0:sl] = h_re
                s_ref[rows, sl:2 * sl] = h_im

    for t0 in range(0, t_len, tg):
        for t in range(t0, t0 + tg):
            x_ref[t] = jnp.dot(u_ref[0, t], bb_ref[0], preferred_element_type=F32)
        advance(t0, t0 == 0, False)

    at_re, at_im = a_ref[0, 2:3, :], a_ref[0, 3:4, :]

    def chunk_scan(i, h):
        h_re, h_im = h
        row = pl.ds(i, 1)
        hp_ref[row, 0:sl] = h_re
        hp_ref[row, sl:2 * sl] = h_im
        s_re, s_im = s_ref[row, 0:sl], s_ref[row, sl:2 * sl]
        return (at_re * h_re - at_im * h_im + s_re, at_re * h_im + at_im * h_re + s_im)

    zero = jnp.zeros((1, sl), F32)
    lax.fori_loop(0, n, chunk_scan, (zero, zero), unroll=8)

    for t0 in range(0, t_len, tg):
        advance(t0, t0 == 0, True)
        for t in range(t0, t0 + tg):
            y = jnp.dot(x_ref[t].astype(BF16), cc_ref[0], preferred_element_type=F32)
            y = y + d_ref[0] * u_ref[0, t].astype(F32)
            o_ref[0, t] = _gelu_tanh(y).astype(o_ref.dtype)


def _ssm_tables(lam_re, lam_im, log_dt, b_re, b_im, c_re, c_im, d_skip):
    t_len = SSM_CHUNK
    depth, groups, pst = lam_re.shape
    gc, gp = SSM_GROUP, SSM_GROUPS_PER_STEP
    nb = groups // gp
    dt = jnp.exp(log_dt)[..., None]
    ar, ai = lam_re * dt, lam_im * dt
    a_re, a_im = jnp.exp(ar) * jnp.cos(ai), jnp.exp(ar) * jnp.sin(ai)
    at_re = jnp.exp(t_len * ar) * jnp.cos(t_len * ai)
    at_im = jnp.exp(t_len * ar) * jnp.sin(t_len * ai)
    den = lam_re * lam_re + lam_im * lam_im
    nr, ni = a_re - 1.0, a_im
    f_re = (nr * lam_re + ni * lam_im) / den
    f_im = (ni * lam_re - nr * lam_im) / den
    bb_re = f_re[..., None] * b_re - f_im[..., None] * b_im
    bb_im = f_re[..., None] * b_im + f_im[..., None] * b_re
    eye = jnp.eye(gp, dtype=F32)

    def in_block(m):
        m = m.reshape(depth, nb, gp, pst, gc).transpose(0, 1, 2, 4, 3)
        return (m[:, :, :, :, None, :] * eye[None, None, :, None, :, None]).reshape(depth, nb, gp * gc, gp * pst)

    def out_block(m):
        m = m.reshape(depth, nb, gp, gc, pst).transpose(0, 1, 2, 4, 3)
        return (m[:, :, :, :, None, :] * eye[None, None, :, None, :, None]).reshape(depth, nb, gp * pst, gp * gc)

    bb_t = jnp.concatenate([in_block(bb_re), in_block(bb_im)], axis=3).astype(BF16)
    cc_t = jnp.concatenate([out_block(c_re), out_block(-c_im)], axis=2).astype(BF16)
    lanes = lambda m: m.reshape(depth, nb, gp * pst)
    a_t = jnp.stack([lanes(a_re), lanes(a_im), lanes(at_re), lanes(at_im)], axis=2)
    d_t = d_skip.reshape(depth, nb, 1, gp * gc)
    return bb_t, cc_t, a_t, d_t


def _ssm_branch(u, tabs, layer, bsz, seq):
    bb_t, cc_t, a_t, d_t = tabs
    _, nb, win, sl2 = bb_t.shape
    t_len = SSM_CHUNK
    n = seq // t_len
    rb = min(SSM_ROW_BLOCK, n)
    width = u.shape[1]
    ur = u.reshape(bsz, n, t_len, width).transpose(0, 2, 1, 3)
    blk = pl.BlockSpec((1, t_len, n, win), lambda b, k: (b, 0, 0, k))
    tab = lambda a: pl.BlockSpec((None, 1) + a.shape[2:], lambda b, k: (layer, k, 0, 0))
    g = pl.pallas_call(
        functools.partial(_ssm_kernel, t_len=t_len, rb=rb),
        grid=(bsz, nb),
        in_specs=[blk, tab(bb_t), tab(cc_t), tab(a_t), tab(d_t)],
        out_specs=blk,
        out_shape=jax.ShapeDtypeStruct((bsz, t_len, n, width), BF16),
        scratch_shapes=[pltpu.VMEM((t_len, n, sl2), F32), pltpu.VMEM((n, sl2), F32),
                        pltpu.VMEM((n, sl2), F32)],
        compiler_params=_cparams(("parallel", "arbitrary")),
        name="ssm",
    )(ur, bb_t, cc_t, a_t, d_t)
    return g.transpose(0, 2, 1, 3).reshape(bsz * seq, width)


def _mem_kernel(q_ref, k_ref, v_ref, o_ref):
    for h in range(MEM_HEADS):
        sl = slice(h * HEAD_DIM, (h + 1) * HEAD_DIM)
        s = lax.dot_general(q_ref[:, sl], k_ref[:, sl], (((1,), (1,)), ((), ())),
                            preferred_element_type=F32)
        p = jnp.exp(s - jnp.max(s, axis=-1, keepdims=True))
        l = jnp.sum(p, axis=-1, keepdims=True)
        o = jnp.dot(p.astype(BF16), v_ref[:, sl], preferred_element_type=F32)
        o_ref[:, sl] = (o / l).astype(o_ref.dtype)


def _mem_attention(proj, kv, bsz, seq, n_mem, q_col_block):
    tq = min(1024, seq)
    nq = seq // tq
    return pl.pallas_call(
        _mem_kernel,
        grid=(bsz, nq),
        in_specs=[
            pl.BlockSpec((tq, MEM_WIDTH), lambda b, i: (b * nq + i, q_col_block)),
            pl.BlockSpec((n_mem, MEM_WIDTH), lambda b, i: (b, 0)),
            pl.BlockSpec((n_mem, MEM_WIDTH), lambda b, i: (b, 1)),
        ],
        out_specs=pl.BlockSpec((tq, MEM_WIDTH), lambda b, i: (b * nq + i, 0)),
        out_shape=jax.ShapeDtypeStruct((bsz * seq, MEM_WIDTH), BF16),
        compiler_params=_cparams(("parallel", "arbitrary")),
        name="mem_attention",
    )(proj, kv, kv)


def _layer_norm(xf, g, b):
    mu = jnp.mean(xf, axis=-1, keepdims=True)
    xc = xf - mu
    var = jnp.mean(xc * xc, axis=-1, keepdims=True)
    return xc * lax.rsqrt(var + LN_EPS) * g + b


def _merge_kernel(sb_ref, g_ref, mm_ref, g0_ref, g1_ref, g2_ref, x_ref,
                  wsb_ref, wglu_ref, wsso_ref, wmem_ref, wo_ref, lng_ref, lnb_ref,
                  of_ref, ob_ref, *, alpha, ssm_width):
    p_sb = jnp.dot(sb_ref[...], wsb_ref[...], preferred_element_type=F32)
    glu = jnp.dot(g_ref[...], wglu_ref[...], preferred_element_type=F32)
    gated = glu[:, :ssm_width] * _sigmoid(glu[:, ssm_width:])
    p_ssm = jnp.dot(gated.astype(BF16), wsso_ref[...], preferred_element_type=F32)
    p_mem = jnp.dot(mm_ref[...], wmem_ref[...], preferred_element_type=F32)
    merged = (g0_ref[...].astype(F32) * p_sb + g1_ref[...].astype(F32) * p_ssm
              + g2_ref[...].astype(F32) * p_mem)
    mix = jnp.dot(merged.astype(BF16), wo_ref[...], preferred_element_type=F32)
    y = _layer_norm(alpha * x_ref[...] + mix, lng_ref[...], lnb_ref[...])
    of_ref[...] = y
    ob_ref[...] = y.astype(BF16)


def _layer_spec(a, layer):
    return pl.BlockSpec((None,) + a.shape[1:], lambda i: (layer,) + (0,) * (a.ndim - 1),
                        pipeline_mode=pl.Buffered(1))


def _merge(sb, g, mm, proj, x, wsb, wglu, wsso, wmem, wo, lng, lnb, layer, *, alpha, gate_col_block, tm):
    m, d = x.shape
    ssm_width = g.shape[1]
    row = lambda w: pl.BlockSpec((tm, w), lambda i: (i, 0))
    gate = lambda k: pl.BlockSpec((tm, d), lambda i: (i, gate_col_block + k))
    return pl.pallas_call(
        functools.partial(_merge_kernel, alpha=alpha, ssm_width=ssm_width),
        grid=(m // tm,),
        in_specs=[row(sb.shape[1]), row(ssm_width), row(mm.shape[1]), gate(0), gate(1), gate(2), row(d),
                  *[_layer_spec(a, layer) for a in (wsb, wglu, wsso, wmem, wo, lng, lnb)]],
        out_specs=[row(d), row(d)],
        out_shape=[jax.ShapeDtypeStruct((m, d), F32), jax.ShapeDtypeStruct((m, d), BF16)],
        compiler_params=_cparams(("parallel",)),
        name="merge",
    )(sb, g, mm, proj, proj, proj, x, wsb, wglu, wsso, wmem, wo, lng, lnb)


def _ffn_kernel(xb_ref, x_ref, wg_ref, wu_ref, wd_ref, lng_ref, lnb_ref, of_ref, ob_ref, *, alpha):
    j = pl.program_id(1)

    @pl.when(j == 0)
    def _():
        of_ref[...] = alpha * x_ref[...]

    xb = xb_ref[...]
    fg = jnp.dot(xb, wg_ref[...], preferred_element_type=F32)
    fu = jnp.dot(xb, wu_ref[...], preferred_element_type=F32)
    h = (fg * _sigmoid(fg) * fu).astype(BF16)
    of_ref[...] += jnp.dot(h, wd_ref[...], preferred_element_type=F32)

    @pl.when(j == pl.num_programs(1) - 1)
    def _():
        y = _layer_norm(of_ref[...], lng_ref[...], lnb_ref[...])
        of_ref[...] = y
        ob_ref[...] = y.astype(BF16)


def _ffn(xb, x, wgu, wd, lng, lnb, layer, *, alpha, tm, tf):
    m, d = x.shape
    d_ff = wd.shape[1]
    nf = d_ff // tf
    return pl.pallas_call(
        functools.partial(_ffn_kernel, alpha=alpha),
        grid=(m // tm, nf),
        in_specs=[
            pl.BlockSpec((tm, d), lambda i, j: (i, 0)),
            pl.BlockSpec((tm, d), lambda i, j: (i, 0)),
            pl.BlockSpec((None, d, tf), lambda i, j: (layer, 0, j)),
            pl.BlockSpec((None, d, tf), lambda i, j: (layer, 0, nf + j)),
            pl.BlockSpec((None, tf, d), lambda i, j: (layer, j, 0)),
            pl.BlockSpec((None, 1, d), lambda i, j: (layer, 0, 0)),
            pl.BlockSpec((None, 1, d), lambda i, j: (layer, 0, 0)),
        ],
        out_specs=[pl.BlockSpec((tm, d), lambda i, j: (i, 0)), pl.BlockSpec((tm, d), lambda i, j: (i, 0))],
        out_shape=[jax.ShapeDtypeStruct((m, d), F32), jax.ShapeDtypeStruct((m, d), BF16)],
        compiler_params=_cparams(("parallel", "arbitrary")),
        name="ffn",
    )(xb, x, wgu, wgu, wd, lng, lnb)


def _forward(x, mem, w_in, b_in, sb_w_out, ssm_lambda_re, ssm_lambda_im, ssm_log_dt,
             ssm_b_re, ssm_b_im, ssm_c_re, ssm_c_im, ssm_d, ssm_w_glu, ssm_w_out,
             mem_w_kv, mem_w_out, w_o, ln1_g, ln1_b, ffn_w_gate_up, ffn_w_down,
             ln2_g, ln2_b, *, alpha):
    bsz, seq, d = x.shape
    depth = w_in.shape[0]
    n_mem = mem.shape[1]
    in_width = w_in.shape[2]
    ssm_width = ssm_d.shape[1]
    m = bsz * seq
    tn = 2048
    tm = min(1024, m)
    q_scale = HEAD_DIM ** -0.5

    u_off = 3 * SB_WIDTH
    qm_off = u_off + ssm_width
    gate_off = qm_off + MEM_WIDTH
    assert gate_off % tn == 0 and gate_off % d == 0 and qm_off % MEM_WIDTH == 0
    assert in_width == gate_off + N_BRANCHES * d
    col = jnp.arange(in_width)
    scale_vec = jnp.where((col < SB_WIDTH) | ((col >= qm_off) & (col < gate_off)), q_scale, 1.0)
    scale_vec = scale_vec.astype(F32)[None, :]

    tabs = _ssm_tables(ssm_lambda_re, ssm_lambda_im, ssm_log_dt, ssm_b_re, ssm_b_im,
                       ssm_c_re, ssm_c_im, ssm_d)
    bf = lambda a: a.astype(BF16)
    row3 = lambda a: a[:, None, :]
    w_in_b, wkv_b, wsb_b, wglu_b, wsso_b = bf(w_in), bf(mem_w_kv), bf(sb_w_out), bf(ssm_w_glu), bf(ssm_w_out)
    wmem_b, wo_b, wgu_b, wd_b = bf(mem_w_out), bf(w_o), bf(ffn_w_gate_up), bf(ffn_w_down)
    b_in3, ln1_g3, ln1_b3, ln2_g3, ln2_b3 = row3(b_in), row3(ln1_g), row3(ln1_b), row3(ln2_g), row3(ln2_b)
    mem_b = bf(mem.reshape(bsz * n_mem, d))
    kv_width = mem_w_kv.shape[2]
    kv_zero = jnp.zeros((depth, 1, kv_width), F32)
    kv_one = jnp.ones((1, kv_width), F32)

    xf = x.reshape(m, d)
    xb = xf
    for l in range(depth):
        proj = _proj(xb, w_in_b, b_in3, scale_vec, l, tm=tm, tn=tn, n_plain=gate_off // tn)
        sb = _sb_attention(proj, bsz, seq)
        g = _ssm_branch(proj[:, u_off:qm_off], tabs, l, bsz, seq)
        kv = _proj(mem_b, wkv_b, kv_zero, kv_one, l, tm=min(1024, bsz * n_mem), tn=kv_width, n_plain=1)
        mm = _mem_attention(proj, kv, bsz, seq, n_mem, qm_off // MEM_WIDTH)
        xf, xb = _merge(sb, g, mm, proj, xf, wsb_b, wglu_b, wsso_b, wmem_b, wo_b, ln1_g3, ln1_b3, l,
                        alpha=alpha, gate_col_block=gate_off // d, tm=min(256, m))
        xf, xb = _ffn(xb, xf, wgu_b, wd_b, ln2_g3, ln2_b3, l, alpha=alpha, tm=min(512, m), tf=512)
    return xf.reshape(bsz, seq, d)


def kernel(x, mem, w_in, b_in, sb_w_out, ssm_lambda_re, ssm_lambda_im, ssm_log_dt, ssm_b_re, ssm_b_im, ssm_c_re, ssm_c_im, ssm_d, ssm_w_glu, ssm_w_out, mem_w_kv, mem_w_out, w_o, ln1_g, ln1_b, ffn_w_gate_up, ffn_w_down, ln2_g, ln2_b):
    depth = w_in.shape[0]
    alpha = (2 * depth) ** 0.25
    return _forward(x, mem, w_in, b_in, sb_w_out, ssm_lambda_re, ssm_lambda_im, ssm_log_dt,
                    ssm_b_re, ssm_b_im, ssm_c_re, ssm_c_im, ssm_d, ssm_w_glu, ssm_w_out,
                    mem_w_kv, mem_w_out, w_o, ln1_g, ln1_b, ffn_w_gate_up, ffn_w_down,
                    ln2_g, ln2_b, alpha=alpha)
```

```python
import functools
import math

import jax
import jax.numpy as jnp
from jax import lax
from jax.experimental import pallas as pl
from jax.experimental.pallas import tpu as pltpu

F32 = jnp.float32
BF16 = jnp.bfloat16

SB_HEADS = 8
HEAD_DIM = 128
SB_WIDTH = SB_HEADS * HEAD_DIM
SSM_GROUP = 16
SSM_STATE = 64
MEM_HEADS = 4
MEM_WIDTH = MEM_HEADS * HEAD_DIM
N_BRANCHES = 3
LN_EPS = 1e-5

LANES = 128

SSM_CHUNK = 16
SSM_ROW_BLOCK = 16
SSM_TIME_GROUP = 4
SSM_GROUPS_PER_STEP = LANES // SSM_GROUP

SB_TQ = 256
SB_TK = 128
SB_LOG_CUTOFF = -110.0
SB_STATIC_TILES = 2
SB_HEADS_PER_STEP = 4

VMEM_LIMIT = 56 * 1024 * 1024


def _cparams(sem):
    return pltpu.CompilerParams(dimension_semantics=sem, vmem_limit_bytes=VMEM_LIMIT)


def _sigmoid(x):
    return 1.0 / (1.0 + jnp.exp(-x))


def _proj_kernel(x_ref, w_ref, b_ref, o_ref, *, n_plain):
    acc = jnp.dot(x_ref[...].astype(BF16), w_ref[...], preferred_element_type=F32)
    acc = acc + b_ref[...]
    is_gate = pl.program_id(1) >= n_plain
    o_ref[...] = jnp.where(is_gate, _sigmoid(acc), acc).astype(o_ref.dtype)


def _proj(x, w, b, layer, *, tm, tn, n_plain):
    m, k = x.shape
    n = w.shape[2]
    return pl.pallas_call(
        functools.partial(_proj_kernel, n_plain=n_plain),
        grid=(m // tm, n // tn),
        in_specs=[
            pl.BlockSpec((tm, k), lambda i, j: (i, 0)),
            pl.BlockSpec((None, k, tn), lambda i, j: (layer, 0, j)),
            pl.BlockSpec((None, 1, tn), lambda i, j: (layer, 0, j)),
        ],
        out_specs=pl.BlockSpec((tm, tn), lambda i, j: (i, j)),
        out_shape=jax.ShapeDtypeStruct((m, n), BF16),
        compiler_params=_cparams(("parallel", "arbitrary")),
        name="proj",
    )(x, w, b)


def _sb_kernel(q_ref, k_ref, v_ref, tri_ref, o_ref, acc_ref, carry_ref, z_ref, cs_ref, *, tq, tk, n_static):
    nq = q_ref.shape[0] // tq
    n_diag = tq // tk
    heads = range(q_ref.shape[1] // HEAD_DIM)
    lanes = lambda h: slice(h * HEAD_DIM, (h + 1) * HEAD_DIM)

    def tile(q, h, j, carry):
        start = pl.multiple_of(j * tk, tk)
        k = k_ref[pl.ds(start, tk), lanes(h)]
        v = v_ref[pl.ds(start, tk), lanes(h)]
        z = lax.dot_general(q, k, (((1,), (1,)), ((), ())), preferred_element_type=F32)
        sp = jnp.maximum(z, 0.0) + jnp.log(1.0 + jnp.exp(-jnp.abs(z)))
        cs = jnp.dot(sp.astype(BF16), tri_ref[...], preferred_element_type=F32)
        incl = cs[:, :tk]
        tot = cs[:, tk:]
        w = jnp.exp(z + incl + carry)
        return jnp.dot(w.astype(BF16), v, preferred_element_type=F32), carry + tot

    def causal_mask(w, n_st):
        d = w - n_st
        if d < 0:
            return None
        row = lax.broadcasted_iota(jnp.int32, (tq, tk), 0)
        col = lax.broadcasted_iota(jnp.int32, (tq, tk), 1)
        return col + d * tk < row

    def window_scores(i, h, n_st):
        n_win = n_st + n_diag
        q = q_ref[pl.ds(pl.multiple_of(i * tq, tq), tq), lanes(h)]
        start = pl.multiple_of((i * n_diag - n_st) * tk, tk)
        kw = k_ref[pl.ds(start, n_win * tk), lanes(h)]
        z = lax.dot_general(q, kw, (((1,), (1,)), ((), ())), preferred_element_type=F32)
        hl = []
        for w in range(n_win):
            zw = z[:, w * tk:(w + 1) * tk]
            sp = jnp.maximum(zw, 0.0) + jnp.log(1.0 + jnp.exp(-jnp.abs(zw)))
            cw = causal_mask(w, n_st)
            if cw is not None:
                sp = jnp.where(cw, sp, 0.0)
            hl.append(sp.astype(BF16))
        cs = jnp.dot(jnp.concatenate(hl, axis=0), tri_ref[...], preferred_element_type=F32)
        return z, cs

    def window_output(i, h, n_st, z, cs):
        n_win = n_st + n_diag
        start = pl.multiple_of((i * n_diag - n_st) * tk, tk)
        vw = v_ref[pl.ds(start, n_win * tk), lanes(h)]
        carry = jnp.zeros((tq, tk), F32)
        ws = [None] * n_win
        for w in reversed(range(n_win)):
            incl = cs[w * tq:(w + 1) * tq, :tk]
            tot = cs[w * tq:(w + 1) * tq, tk:]
            ww = jnp.exp(z[:, w * tk:(w + 1) * tk] + incl + carry)
            cw = causal_mask(w, n_st)
            if cw is not None:
                ww = jnp.where(cw, ww, 0.0)
            ws[w] = ww.astype(BF16)
            carry = carry + tot
        acc = jnp.dot(jnp.concatenate(ws, axis=1), vw, preferred_element_type=F32)
        return acc, carry

    def qtile(i, n_st, pipelined):
        row0 = pl.multiple_of(i * tq, tq)
        carry = []
        for h in heads:
            if pipelined:
                z, cs = z_ref[h], cs_ref[h]
            else:
                z, cs = window_scores(i, h, n_st)
            acc_h, carry_h = window_output(i, h, n_st, z, cs)
            acc_ref[h] = acc_h
            carry_ref[h] = carry_h
            carry.append(carry_h)
        if pipelined:
            nxt = jnp.minimum(i + 1, nq - 1)
            for h in heads:
                z_ref[h], cs_ref[h] = window_scores(nxt, h, n_st)

        def cond(c):
            j, m = c
            return jnp.logical_and(j >= 0, m > SB_LOG_CUTOFF)

        def body(c):
            j, _ = c
            m = None
            for h in heads:
                pv, new_carry = tile(q_ref[pl.ds(row0, tq), lanes(h)], h, j, carry_ref[h])
                acc_ref[h] += pv
                carry_ref[h] = new_carry
                mh = jnp.max(new_carry)
                m = mh if m is None else jnp.maximum(m, mh)
            return j - 1, m

        m0 = functools.reduce(jnp.maximum, [jnp.max(c) for c in carry])
        lax.while_loop(cond, body, (i * n_diag - 1 - n_st, m0))
        for h in heads:
            o_ref[pl.ds(row0, tq), lanes(h)] = acc_ref[h].astype(o_ref.dtype)

    qtile(0, 0, False)
    if nq > 1:
        n_st = n_static
        for h in heads:
            z_ref[h], cs_ref[h] = window_scores(1, h, n_st)

        def qbody(i, c):
            qtile(i, n_st, True)
            return c

        lax.fori_loop(1, nq, qbody, 0)


def _sb_prefix_matrix(tk):
    j = jnp.arange(tk)[:, None]
    s = jnp.arange(tk)[None, :]
    return -jnp.concatenate([(j >= s).astype(F32), jnp.ones((tk, tk), F32)], axis=1).astype(BF16)


def _sb_attention(proj, bsz, seq):
    tq, tk = min(SB_TQ, seq), min(SB_TK, seq)
    tri = _sb_prefix_matrix(tk)
    n_static = min(SB_STATIC_TILES, tq // tk)
    n_win = n_static + tq // tk
    hp = SB_HEADS_PER_STEP
    ng = SB_HEADS // hp
    col = lambda off: pl.BlockSpec((seq, hp * HEAD_DIM), lambda b, h: (b, off + h))
    return pl.pallas_call(
        functools.partial(_sb_kernel, tq=tq, tk=tk, n_static=n_static),
        grid=(bsz, ng),
        in_specs=[col(0), col(ng), col(2 * ng),
                  pl.BlockSpec((tk, 2 * tk), lambda b, h: (0, 0))],
        out_specs=pl.BlockSpec((seq, hp * HEAD_DIM), lambda b, h: (b, h)),
        out_shape=jax.ShapeDtypeStruct((bsz * seq, SB_WIDTH), BF16),
        scratch_shapes=[pltpu.VMEM((hp, tq, HEAD_DIM), F32), pltpu.VMEM((hp, tq, tk), F32),
                        pltpu.VMEM((hp, tq, n_win * tk), F32), pltpu.VMEM((hp, n_win * tq, 2 * tk), F32)],
        compiler_params=_cparams(("parallel", "arbitrary")),
        name="sb_attention",
    )(proj, proj, proj, tri)


def _gelu_tanh(x):
    c = math.sqrt(2.0 / math.pi)
    return 0.5 * x * (1.0 + jnp.tanh(c * (x + 0.044715 * (x * x * x))))


def _ssm_kernel(u_ref, bb_ref, cc_ref, a_ref, d_ref, o_ref, x_ref, s_ref, hp_ref, *, t_len, rb):
    n = u_ref.shape[2]
    sl = a_ref.shape[-1]
    a_re = jnp.broadcast_to(a_ref[0, 0:1, :], (rb, sl))
    a_im = jnp.broadcast_to(a_ref[0, 1:2, :], (rb, sl))
    row_blocks = [slice(r * rb, (r + 1) * rb) for r in range(n // rb)]

    tg = SSM_TIME_GROUP

    def advance(t0, first, keep):
        for rows in row_blocks:
            if first and keep:
                h_re, h_im = hp_ref[rows, 0:sl], hp_ref[rows, sl:2 * sl]
            elif first:
                h_re = h_im = jnp.zeros((rb, sl), F32)
            elif keep:
                h_re, h_im = x_ref[t0 - 1, rows, 0:sl], x_ref[t0 - 1, rows, sl:2 * sl]
            else:
                h_re, h_im = s_ref[rows, 0:sl], s_ref[rows, sl:2 * sl]
            for t in range(t0, t0 + tg):
                x_re, x_im = x_ref[t, rows, 0:sl], x_ref[t, rows, sl:2 * sl]
                h_re, h_im = (a_re * h_re - a_im * h_im + x_re, a_re * h_im + a_im * h_re + x_im)
                if keep:
                    x_ref[t, rows, 0:sl] = h_re
                    x_ref[t, rows, sl:2 * sl] = h_im
            if not keep:
                s_ref[rows, 0:sl] = h_re
                s_ref[rows, sl:2 * sl] = h_im

    for t0 in range(0, t_len, tg):
        for t in range(t0, t0 + tg):
            x_ref[t] = jnp.dot(u_ref[0, t], bb_ref[0], preferred_element_type=F32)
        advance(t0, t0 == 0, False)

    at_re, at_im = a_ref[0, 2:3, :], a_ref[0, 3:4, :]

    def chunk_scan(i, h):
        h_re, h_im = h
        row = pl.ds(i, 1)
        hp_ref[row, 0:sl] = h_re
        hp_ref[row, sl:2 * sl] = h_im
        s_re, s_im = s_ref[row, 0:sl], s_ref[row, sl:2 * sl]
        return (at_re * h_re - at_im * h_im + s_re, at_re * h_im + at_im * h_re + s_im)

    zero = jnp.zeros((1, sl), F32)
    lax.fori_loop(0, n, chunk_scan, (zero, zero), unroll=8)

    for t0 in range(0, t_len, tg):
        advance(t0, t0 == 0, True)
        for t in range(t0, t0 + tg):
            y = jnp.dot(x_ref[t].astype(BF16), cc_ref[0], preferred_element_type=F32)
            y = y + d_ref[0] * u_ref[0, t].astype(F32)
            o_ref[0, t] = _gelu_tanh(y).astype(o_ref.dtype)


def _ssm_tables(lam_re, lam_im, log_dt, b_re, b_im, c_re, c_im, d_skip):
    t_len = SSM_CHUNK
    depth, groups, pst = lam_re.shape
    gc, gp = SSM_GROUP, SSM_GROUPS_PER_STEP
    nb = groups // gp
    dt = jnp.exp(log_dt)[..., None]
    ar, ai = lam_re * dt, lam_im * dt
    a_re, a_im = jnp.exp(ar) * jnp.cos(ai), jnp.exp(ar) * jnp.sin(ai)
    at_re = jnp.exp(t_len * ar) * jnp.cos(t_len * ai)
    at_im = jnp.exp(t_len * ar) * jnp.sin(t_len * ai)
    den = lam_re * lam_re + lam_im * lam_im
    nr, ni = a_re - 1.0, a_im
    f_re = (nr * lam_re + ni * lam_im) / den
    f_im = (ni * lam_re - nr * lam_im) / den
    bb_re = f_re[..., None] * b_re - f_im[..., None] * b_im
    bb_im = f_re[..., None] * b_im + f_im[..., None] * b_re
    eye = jnp.eye(gp, dtype=F32)

    def in_block(m):
        m = m.reshape(depth, nb, gp, pst, gc).transpose(0, 1, 2, 4, 3)
        return (m[:, :, :, :, None, :] * eye[None, None, :, None, :, None]).reshape(depth, nb, gp * gc, gp * pst)

    def out_block(m):
        m = m.reshape(depth, nb, gp, gc, pst).transpose(0, 1, 2, 4, 3)
        return (m[:, :, :, :, None, :] * eye[None, None, :, None, :, None]).reshape(depth, nb, gp * pst, gp * gc)

    bb_t = jnp.concatenate([in_block(bb_re), in_block(bb_im)], axis=3).astype(BF16)
    cc_t = jnp.concatenate([out_block(c_re), out_block(-c_im)], axis=2).astype(BF16)
    lanes = lambda m: m.reshape(depth, nb, gp * pst)
    a_t = jnp.stack([lanes(a_re), lanes(a_im), lanes(at_re), lanes(at_im)], axis=2)
    d_t = d_skip.reshape(depth, nb, 1, gp * gc)
    return bb_t, cc_t, a_t, d_t


def _ssm_branch(u, tabs, layer, bsz, seq):
    bb_t, cc_t, a_t, d_t = tabs
    _, nb, win, sl2 = bb_t.shape
    t_len = SSM_CHUNK
    n = seq // t_len
    rb = min(SSM_ROW_BLOCK, n)
    width = u.shape[1]
    ur = u.reshape(bsz, n, t_len, width).transpose(0, 2, 1, 3)
    blk = pl.BlockSpec((1, t_len, n, win), lambda b, k: (b, 0, 0, k))
    tab = lambda a: pl.BlockSpec((None, 1) + a.shape[2:], lambda b, k: (layer, k, 0, 0))
    g = pl.pallas_call(
        functools.partial(_ssm_kernel, t_len=t_len, rb=rb),
        grid=(bsz, nb),
        in_specs=[blk, tab(bb_t), tab(cc_t), tab(a_t), tab(d_t)],
        out_specs=blk,
        out_shape=jax.ShapeDtypeStruct((bsz, t_len, n, width), BF16),
        scratch_shapes=[pltpu.VMEM((t_len, n, sl2), F32), pltpu.VMEM((n, sl2), F32),
                        pltpu.VMEM((n, sl2), F32)],
        compiler_params=_cparams(("parallel", "arbitrary")),
        name="ssm",
    )(ur, bb_t, cc_t, a_t, d_t)
    return g.transpose(0, 2, 1, 3).reshape(bsz * seq, width)


def _mem_kernel(q_ref, k_ref, v_ref, o_ref):
    for h in range(MEM_HEADS):
        sl = slice(h * HEAD_DIM, (h + 1) * HEAD_DIM)
        s = lax.dot_general(q_ref[:, sl], k_ref[:, sl], (((1,), (1,)), ((), ())),
                            preferred_element_type=F32)
        p = jnp.exp(s - jnp.max(s, axis=-1, keepdims=True))
        l = jnp.sum(p, axis=-1, keepdims=True)
        o = jnp.dot(p.astype(BF16), v_ref[:, sl], preferred_element_type=F32)
        o_ref[:, sl] = (o / l).astype(o_ref.dtype)


def _mem_attention(proj, kv, bsz, seq, n_mem, q_col_block):
    tq = min(1024, seq)
    nq = seq // tq
    return pl.pallas_call(
        _mem_kernel,
        grid=(bsz, nq),
        in_specs=[
            pl.BlockSpec((tq, MEM_WIDTH), lambda b, i: (b * nq + i, q_col_block)),
            pl.BlockSpec((n_mem, MEM_WIDTH), lambda b, i: (b, 0)),
            pl.BlockSpec((n_mem, MEM_WIDTH), lambda b, i: (b, 1)),
        ],
        out_specs=pl.BlockSpec((tq, MEM_WIDTH), lambda b, i: (b * nq + i, 0)),
        out_shape=jax.ShapeDtypeStruct((bsz * seq, MEM_WIDTH), BF16),
        compiler_params=_cparams(("parallel", "arbitrary")),
        name="mem_attention",
    )(proj, kv, kv)


def _layer_norm(xf, g, b):
    mu = jnp.mean(xf, axis=-1, keepdims=True)
    xc = xf - mu
    var = jnp.mean(xc * xc, axis=-1, keepdims=True)
    return xc * lax.rsqrt(var + LN_EPS) * g + b


def _merge_kernel(sb_ref, g_ref, mm_ref, g0_ref, g1_ref, g2_ref, x_ref,
                  wsb_ref, wglu_ref, wsso_ref, wmem_ref, wo_ref, lng_ref, lnb_ref,
                  of_ref, ob_ref, *, alpha, ssm_width):
    p_sb = jnp.dot(sb_ref[...], wsb_ref[...], preferred_element_type=F32)
    glu = jnp.dot(g_ref[...], wglu_ref[...], preferred_element_type=F32)
    gated = glu[:, :ssm_width] * _sigmoid(glu[:, ssm_width:])
    p_ssm = jnp.dot(gated.astype(BF16), wsso_ref[...], preferred_element_type=F32)
    p_mem = jnp.dot(mm_ref[...], wmem_ref[...], preferred_element_type=F32)
    merged = (g0_ref[...].astype(F32) * p_sb + g1_ref[...].astype(F32) * p_ssm
              + g2_ref[...].astype(F32) * p_mem)
    mix = jnp.dot(merged.astype(BF16), wo_ref[...], preferred_element_type=F32)
    y = _layer_norm(alpha * x_ref[...] + mix, lng_ref[...], lnb_ref[...])
    of_ref[...] = y
    ob_ref[...] = y.astype(BF16)


def _layer_spec(a, layer):
    return pl.BlockSpec((None,) + a.shape[1:], lambda i: (layer,) + (0,) * (a.ndim - 1),
                        pipeline_mode=pl.Buffered(1))


def _merge(sb, g, mm, proj, x, wsb, wglu, wsso, wmem, wo, lng, lnb, layer, *, alpha, gate_col_block, tm):
    m, d = x.shape
    ssm_width = g.shape[1]
    row = lambda w: pl.BlockSpec((tm, w), lambda i: (i, 0))
    gate = lambda k: pl.BlockSpec((tm, d), lambda i: (i, gate_col_block + k))
    return pl.pallas_call(
        functools.partial(_merge_kernel, alpha=alpha, ssm_width=ssm_width),
        grid=(m // tm,),
        in_specs=[row(sb.shape[1]), row(ssm_width), row(mm.shape[1]), gate(0), gate(1), gate(2), row(d),
                  *[_layer_spec(a, layer) for a in (wsb, wglu, wsso, wmem, wo, lng, lnb)]],
        out_specs=[row(d), row(d)],
        out_shape=[jax.ShapeDtypeStruct((m, d), F32), jax.ShapeDtypeStruct((m, d), BF16)],
        compiler_params=_cparams(("parallel",)),
        name="merge",
    )(sb, g, mm, proj, proj, proj, x, wsb, wglu, wsso, wmem, wo, lng, lnb)


def _ffn_kernel(xb_ref, x_ref, wg_ref, wu_ref, wd_ref, lng_ref, lnb_ref, of_ref, ob_ref, *, alpha):
    j = pl.program_id(1)

    @pl.when(j == 0)
    def _():
        of_ref[...] = alpha * x_ref[...]

    xb = xb_ref[...]
    fg = jnp.dot(xb, wg_ref[...], preferred_element_type=F32)
    fu = jnp.dot(xb, wu_ref[...], preferred_element_type=F32)
    h = (fg * _sigmoid(fg) * fu).astype(BF16)
    of_ref[...] += jnp.dot(h, wd_ref[...], preferred_element_type=F32)

    @pl.when(j == pl.num_programs(1) - 1)
    def _():
        y = _layer_norm(of_ref[...], lng_ref[...], lnb_ref[...])
        of_ref[...] = y
        ob_ref[...] = y.astype(BF16)


def _ffn(xb, x, wgu, wd, lng, lnb, layer, *, alpha, tm, tf):
    m, d = x.shape
    d_ff = wd.shape[1]
    nf = d_ff // tf
    return pl.pallas_call(
        functools.partial(_ffn_kernel, alpha=alpha),
        grid=(m // tm, nf),
        in_specs=[
            pl.BlockSpec((tm, d), lambda i, j: (i, 0)),
            pl.BlockSpec((tm, d), lambda i, j: (i, 0)),
            pl.BlockSpec((None, d, tf), lambda i, j: (layer, 0, j)),
            pl.BlockSpec((None, d, tf), lambda i, j: (layer, 0, nf + j)),
            pl.BlockSpec((None, tf, d), lambda i, j: (layer, j, 0)),
            pl.BlockSpec((None, 1, d), lambda i, j: (layer, 0, 0)),
            pl.BlockSpec((None, 1, d), lambda i, j: (layer, 0, 0)),
        ],
        out_specs=[pl.BlockSpec((tm, d), lambda i, j: (i, 0)), pl.BlockSpec((tm, d), lambda i, j: (i, 0))],
        out_shape=[jax.ShapeDtypeStruct((m, d), F32), jax.ShapeDtypeStruct((m, d), BF16)],
        compiler_params=_cparams(("parallel", "arbitrary")),
        name="ffn",
    )(xb, x, wgu, wgu, wd, lng, lnb)


def _forward(x, mem, w_in, b_in, sb_w_out, ssm_lambda_re, ssm_lambda_im, ssm_log_dt,
             ssm_b_re, ssm_b_im, ssm_c_re, ssm_c_im, ssm_d, ssm_w_glu, ssm_w_out,
             mem_w_kv, mem_w_out, w_o, ln1_g, ln1_b, ffn_w_gate_up, ffn_w_down,
             ln2_g, ln2_b, *, alpha):
    bsz, seq, d = x.shape
    depth = w_in.shape[0]
    n_mem = mem.shape[1]
    in_width = w_in.shape[2]
    ssm_width = ssm_d.shape[1]
    m = bsz * seq
    tn = 2048
    tm = min(1024, m)
    q_scale = HEAD_DIM ** -0.5

    u_off = 3 * SB_WIDTH
    qm_off = u_off + ssm_width
    gate_off = qm_off + MEM_WIDTH
    assert gate_off % tn == 0 and gate_off % d == 0 and qm_off % MEM_WIDTH == 0
    assert in_width == gate_off + N_BRANCHES * d
    col = jnp.arange(in_width)
    scale_vec = jnp.where((col < SB_WIDTH) | ((col >= qm_off) & (col < gate_off)), q_scale, 1.0)
    scale_vec = scale_vec.astype(F32)

    tabs = _ssm_tables(ssm_lambda_re, ssm_lambda_im, ssm_log_dt, ssm_b_re, ssm_b_im,
                       ssm_c_re, ssm_c_im, ssm_d)
    bf = lambda a: a.astype(BF16)
    row3 = lambda a: a[:, None, :]
    w_in_b, wkv_b, wsb_b, wglu_b, wsso_b = bf(w_in * scale_vec), bf(mem_w_kv), bf(sb_w_out), bf(ssm_w_glu), bf(ssm_w_out)
    wmem_b, wo_b, wgu_b, wd_b = bf(mem_w_out), bf(w_o), bf(ffn_w_gate_up), bf(ffn_w_down)
    b_in3, ln1_g3, ln1_b3, ln2_g3, ln2_b3 = row3(b_in * scale_vec), row3(ln1_g), row3(ln1_b), row3(ln2_g), row3(ln2_b)
    mem_b = bf(mem.reshape(bsz * n_mem, d))
    kv_width = mem_w_kv.shape[2]
    kv_zero = jnp.zeros((depth, 1, kv_width), F32)

    xf = x.reshape(m, d)
    xb = xf
    for l in range(depth):
        proj = _proj(xb, w_in_b, b_in3, l, tm=tm, tn=tn, n_plain=gate_off // tn)
        sb = _sb_attention(proj, bsz, seq)
        g = _ssm_branch(proj[:, u_off:qm_off], tabs, l, bsz, seq)
        kv = _proj(mem_b, wkv_b, kv_zero, l, tm=min(1024, bsz * n_mem), tn=kv_width, n_plain=1)
        mm = _mem_attention(proj, kv, bsz, seq, n_mem, qm_off // MEM_WIDTH)
        xf, xb = _merge(sb, g, mm, proj, xf, wsb_b, wglu_b, wsso_b, wmem_b, wo_b, ln1_g3, ln1_b3, l,
                        alpha=alpha, gate_col_block=gate_off // d, tm=min(256, m))
        xf, xb = _ffn(xb, xf, wgu_b, wd_b, ln2_g3, ln2_b3, l, alpha=alpha, tm=min(512, m), tf=512)
    return xf.reshape(bsz, seq, d)


def kernel(x, mem, w_in, b_in, sb_w_out, ssm_lambda_re, ssm_lambda_im, ssm_log_dt, ssm_b_re, ssm_b_im, ssm_c_re, ssm_c_im, ssm_d, ssm_w_glu, ssm_w_out, mem_w_kv, mem_w_out, w_o, ln1_g, ln1_b, ffn_w_gate_up, ffn_w_down, ln2_g, ln2_b):
    depth = w_in.shape[0]
    alpha = (2 * depth) ** 0.25
    return _forward(x, mem, w_in, b_in, sb_w_out, ssm_lambda_re, ssm_lambda_im, ssm_log_dt,
                    ssm_b_re, ssm_b_im, ssm_c_re, ssm_c_im, ssm_d, ssm_w_glu, ssm_w_out,
                    mem_w_kv, mem_w_out, w_o, ln1_g, ln1_b, ffn_w_gate_up, ffn_w_down,
                    ln2_g, ln2_b, alpha=alpha)
```

```python
import functools
import math

import jax
import jax.numpy as jnp
from jax import lax
from jax.experimental import pallas as pl
from jax.experimental.pallas import tpu as pltpu

F32 = jnp.float32
BF16 = jnp.bfloat16

SB_HEADS = 8
HEAD_DIM = 128
SB_WIDTH = SB_HEADS * HEAD_DIM
SSM_GROUP = 16
SSM_STATE = 64
MEM_HEADS = 4
MEM_WIDTH = MEM_HEADS * HEAD_DIM
N_BRANCHES = 3
LN_EPS = 1e-5

LANES = 128

SSM_CHUNK = 16
SSM_ROW_BLOCK = 16
SSM_TIME_GROUP = 4
SSM_GROUPS_PER_STEP = LANES // SSM_GROUP

SB_TQ = 256
SB_TK = 128
SB_LOG_CUTOFF = -110.0
SB_STATIC_TILES = 2
SB_HEADS_PER_STEP = 4

VMEM_LIMIT = 56 * 1024 * 1024


def _cparams(sem):
    return pltpu.CompilerParams(dimension_semantics=sem, vmem_limit_bytes=VMEM_LIMIT)


def _sigmoid(x):
    return 0.5 * jnp.tanh(0.5 * x) + 0.5


def _proj_kernel(x_ref, w_ref, b_ref, o_ref, *, n_plain):
    acc = jnp.dot(x_ref[...].astype(BF16), w_ref[...], preferred_element_type=F32)
    acc = acc + b_ref[...]
    is_gate = pl.program_id(1) >= n_plain
    o_ref[...] = jnp.where(is_gate, _sigmoid(acc), acc).astype(o_ref.dtype)


def _proj(x, w, b, layer, *, tm, tn, n_plain):
    m, k = x.shape
    n = w.shape[2]
    return pl.pallas_call(
        functools.partial(_proj_kernel, n_plain=n_plain),
        grid=(m // tm, n // tn),
        in_specs=[
            pl.BlockSpec((tm, k), lambda i, j: (i, 0)),
            pl.BlockSpec((None, k, tn), lambda i, j: (layer, 0, j)),
            pl.BlockSpec((None, 1, tn), lambda i, j: (layer, 0, j)),
        ],
        out_specs=pl.BlockSpec((tm, tn), lambda i, j: (i, j)),
        out_shape=jax.ShapeDtypeStruct((m, n), BF16),
        compiler_params=_cparams(("parallel", "arbitrary")),
        name="proj",
    )(x, w, b)


def _sb_kernel(q_ref, k_ref, v_ref, tri_ref, o_ref, acc_ref, carry_ref, z_ref, cs_ref, *, tq, tk, n_static):
    nq = q_ref.shape[0] // tq
    n_diag = tq // tk
    heads = range(q_ref.shape[1] // HEAD_DIM)
    lanes = lambda h: slice(h * HEAD_DIM, (h + 1) * HEAD_DIM)

    def tile(q, h, j, carry):
        start = pl.multiple_of(j * tk, tk)
        k = k_ref[pl.ds(start, tk), lanes(h)]
        v = v_ref[pl.ds(start, tk), lanes(h)]
        z = lax.dot_general(q, k, (((1,), (1,)), ((), ())), preferred_element_type=F32)
        sp = jnp.maximum(z, 0.0) + jnp.log(1.0 + jnp.exp(-jnp.abs(z)))
        cs = jnp.dot(sp.astype(BF16), tri_ref[...], preferred_element_type=F32)
        incl = cs[:, :tk]
        tot = cs[:, tk:]
        w = jnp.exp(z + incl + carry)
        return jnp.dot(w.astype(BF16), v, preferred_element_type=F32), carry + tot

    def causal_mask(w, n_st):
        d = w - n_st
        if d < 0:
            return None
        row = lax.broadcasted_iota(jnp.int32, (tq, tk), 0)
        col = lax.broadcasted_iota(jnp.int32, (tq, tk), 1)
        return col + d * tk < row

    def window_scores(i, h, n_st):
        n_win = n_st + n_diag
        q = q_ref[pl.ds(pl.multiple_of(i * tq, tq), tq), lanes(h)]
        start = pl.multiple_of((i * n_diag - n_st) * tk, tk)
        kw = k_ref[pl.ds(start, n_win * tk), lanes(h)]
        z = lax.dot_general(q, kw, (((1,), (1,)), ((), ())), preferred_element_type=F32)
        hl = []
        for w in range(n_win):
            zw = z[:, w * tk:(w + 1) * tk]
            sp = jnp.maximum(zw, 0.0) + jnp.log(1.0 + jnp.exp(-jnp.abs(zw)))
            cw = causal_mask(w, n_st)
            if cw is not None:
                sp = jnp.where(cw, sp, 0.0)
            hl.append(sp.astype(BF16))
        cs = jnp.dot(jnp.concatenate(hl, axis=0), tri_ref[...], preferred_element_type=F32)
        return z, cs

    def window_output(i, h, n_st, z, cs):
        n_win = n_st + n_diag
        start = pl.multiple_of((i * n_diag - n_st) * tk, tk)
        vw = v_ref[pl.ds(start, n_win * tk), lanes(h)]
        carry = jnp.zeros((tq, tk), F32)
        ws = [None] * n_win
        for w in reversed(range(n_win)):
            incl = cs[w * tq:(w + 1) * tq, :tk]
            tot = cs[w * tq:(w + 1) * tq, tk:]
            ww = jnp.exp(z[:, w * tk:(w + 1) * tk] + incl + carry)
            cw = causal_mask(w, n_st)
            if cw is not None:
                ww = jnp.where(cw, ww, 0.0)
            ws[w] = ww.astype(BF16)
            carry = carry + tot
        acc = jnp.dot(jnp.concatenate(ws, axis=1), vw, preferred_element_type=F32)
        return acc, carry

    def qtile(i, n_st, pipelined):
        row0 = pl.multiple_of(i * tq, tq)
        carry = []
        for h in heads:
            if pipelined:
                z, cs = z_ref[h], cs_ref[h]
            else:
                z, cs = window_scores(i, h, n_st)
            acc_h, carry_h = window_output(i, h, n_st, z, cs)
            acc_ref[h] = acc_h
            carry_ref[h] = carry_h
            carry.append(carry_h)
        if pipelined:
            nxt = jnp.minimum(i + 1, nq - 1)
            for h in heads:
                z_ref[h], cs_ref[h] = window_scores(nxt, h, n_st)

        def cond(c):
            j, m = c
            return jnp.logical_and(j >= 0, m > SB_LOG_CUTOFF)

        def body(c):
            j, _ = c
            m = None
            for h in heads:
                pv, new_carry = tile(q_ref[pl.ds(row0, tq), lanes(h)], h, j, carry_ref[h])
                acc_ref[h] += pv
                carry_ref[h] = new_carry
                mh = jnp.max(new_carry)
                m = mh if m is None else jnp.maximum(m, mh)
            return j - 1, m

        m0 = functools.reduce(jnp.maximum, [jnp.max(c) for c in carry])
        lax.while_loop(cond, body, (i * n_diag - 1 - n_st, m0))
        for h in heads:
            o_ref[pl.ds(row0, tq), lanes(h)] = acc_ref[h].astype(o_ref.dtype)

    qtile(0, 0, False)
    if nq > 1:
        n_st = n_static
        for h in heads:
            z_ref[h], cs_ref[h] = window_scores(1, h, n_st)

        def qbody(i, c):
            qtile(i, n_st, True)
            return c

        lax.fori_loop(1, nq, qbody, 0)


def _sb_prefix_matrix(tk):
    j = jnp.arange(tk)[:, None]
    s = jnp.arange(tk)[None, :]
    return -jnp.concatenate([(j >= s).astype(F32), jnp.ones((tk, tk), F32)], axis=1).astype(BF16)


def _sb_attention(proj, bsz, seq):
    tq, tk = min(SB_TQ, seq), min(SB_TK, seq)
    tri = _sb_prefix_matrix(tk)
    n_static = min(SB_STATIC_TILES, tq // tk)
    n_win = n_static + tq // tk
    hp = SB_HEADS_PER_STEP
    ng = SB_HEADS // hp
    col = lambda off: pl.BlockSpec((seq, hp * HEAD_DIM), lambda b, h: (b, off + h))
    return pl.pallas_call(
        functools.partial(_sb_kernel, tq=tq, tk=tk, n_static=n_static),
        grid=(bsz, ng),
        in_specs=[col(0), col(ng), col(2 * ng),
                  pl.BlockSpec((tk, 2 * tk), lambda b, h: (0, 0))],
        out_specs=pl.BlockSpec((seq, hp * HEAD_DIM), lambda b, h: (b, h)),
        out_shape=jax.ShapeDtypeStruct((bsz * seq, SB_WIDTH), BF16),
        scratch_shapes=[pltpu.VMEM((hp, tq, HEAD_DIM), F32), pltpu.VMEM((hp, tq, tk), F32),
                        pltpu.VMEM((hp, tq, n_win * tk), F32), pltpu.VMEM((hp, n_win * tq, 2 * tk), F32)],
        compiler_params=_cparams(("parallel", "arbitrary")),
        name="sb_attention",
    )(proj, proj, proj, tri)


def _gelu_tanh(x):
    c = math.sqrt(2.0 / math.pi)
    return 0.5 * x * (1.0 + jnp.tanh(c * (x + 0.044715 * (x * x * x))))


def _ssm_kernel(u_ref, bb_ref, cc_ref, a_ref, d_ref, o_ref, x_ref, s_ref, hp_ref, *, t_len, rb):
    n = u_ref.shape[2]
    sl = a_ref.shape[-1]
    a_re = jnp.broadcast_to(a_ref[0, 0:1, :], (rb, sl))
    a_im = jnp.broadcast_to(a_ref[0, 1:2, :], (rb, sl))
    row_blocks = [slice(r * rb, (r + 1) * rb) for r in range(n // rb)]

    tg = SSM_TIME_GROUP

    def advance(t0, first, keep):
        for rows in row_blocks:
            if first and keep:
                h_re, h_im = hp_ref[rows, 0:sl], hp_ref[rows, sl:2 * sl]
            elif first:
                h_re = h_im = jnp.zeros((rb, sl), F32)
            elif keep:
                h_re, h_im = x_ref[t0 - 1, rows, 0:sl], x_ref[t0 - 1, rows, sl:2 * sl]
            else:
                h_re, h_im = s_ref[rows, 0:sl], s_ref[rows, sl:2 * sl]
            for t in range(t0, t0 + tg):
                x_re, x_im = x_ref[t, rows, 0:sl], x_ref[t, rows, sl:2 * sl]
                h_re, h_im = (a_re * h_re - a_im * h_im + x_re, a_re * h_im + a_im * h_re + x_im)
                if keep:
                    x_ref[t, rows, 0:sl] = h_re
                    x_ref[t, rows, sl:2 * sl] = h_im
            if not keep:
                s_ref[rows, 0:sl] = h_re
                s_ref[rows, sl:2 * sl] = h_im

    for t0 in range(0, t_len, tg):
        for t in range(t0, t0 + tg):
            x_ref[t] = jnp.dot(u_ref[0, t], bb_ref[0], preferred_element_type=F32)
        advance(t0, t0 == 0, False)

    at_re, at_im = a_ref[0, 2:3, :], a_ref[0, 3:4, :]

    def chunk_scan(i, h):
        h_re, h_im = h
        row = pl.ds(i, 1)
        hp_ref[row, 0:sl] = h_re
        hp_ref[row, sl:2 * sl] = h_im
        s_re, s_im = s_ref[row, 0:sl], s_ref[row, sl:2 * sl]
        return (at_re * h_re - at_im * h_im + s_re, at_re * h_im + at_im * h_re + s_im)

    zero = jnp.zeros((1, sl), F32)
    lax.fori_loop(0, n, chunk_scan, (zero, zero), unroll=8)

    for t0 in range(0, t_len, tg):
        advance(t0, t0 == 0, True)
        for t in range(t0, t0 + tg):
            y = jnp.dot(x_ref[t].astype(BF16), cc_ref[0], preferred_element_type=F32)
            y = y + d_ref[0] * u_ref[0, t].astype(F32)
            o_ref[0, t] = _gelu_tanh(y).astype(o_ref.dtype)


def _ssm_tables(lam_re, lam_im, log_dt, b_re, b_im, c_re, c_im, d_skip):
    t_len = SSM_CHUNK
    depth, groups, pst = lam_re.shape
    gc, gp = SSM_GROUP, SSM_GROUPS_PER_STEP
    nb = groups // gp
    dt = jnp.exp(log_dt)[..., None]
    ar, ai = lam_re * dt, lam_im * dt
    a_re, a_im = jnp.exp(ar) * jnp.cos(ai), jnp.exp(ar) * jnp.sin(ai)
    at_re = jnp.exp(t_len * ar) * jnp.cos(t_len * ai)
    at_im = jnp.exp(t_len * ar) * jnp.sin(t_len * ai)
    den = lam_re * lam_re + lam_im * lam_im
    nr, ni = a_re - 1.0, a_im
    f_re = (nr * lam_re + ni * lam_im) / den
    f_im = (ni * lam_re - nr * lam_im) / den
    bb_re = f_re[..., None] * b_re - f_im[..., None] * b_im
    bb_im = f_re[..., None] * b_im + f_im[..., None] * b_re
    eye = jnp.eye(gp, dtype=F32)

    def in_block(m):
        m = m.reshape(depth, nb, gp, pst, gc).transpose(0, 1, 2, 4, 3)
        return (m[:, :, :, :, None, :] * eye[None, None, :, None, :, None]).reshape(depth, nb, gp * gc, gp * pst)

    def out_block(m):
        m = m.reshape(depth, nb, gp, gc, pst).transpose(0, 1, 2, 4, 3)
        return (m[:, :, :, :, None, :] * eye[None, None, :, None, :, None]).reshape(depth, nb, gp * pst, gp * gc)

    bb_t = jnp.concatenate([in_block(bb_re), in_block(bb_im)], axis=3).astype(BF16)
    cc_t = jnp.concatenate([out_block(c_re), out_block(-c_im)], axis=2).astype(BF16)
    lanes = lambda m: m.reshape(depth, nb, gp * pst)
    a_t = jnp.stack([lanes(a_re), lanes(a_im), lanes(at_re), lanes(at_im)], axis=2)
    d_t = d_skip.reshape(depth, nb, 1, gp * gc)
    return bb_t, cc_t, a_t, d_t


def _ssm_branch(u, tabs, layer, bsz, seq):
    bb_t, cc_t, a_t, d_t = tabs
    _, nb, win, sl2 = bb_t.shape
    t_len = SSM_CHUNK
    n = seq // t_len
    rb = min(SSM_ROW_BLOCK, n)
    width = u.shape[1]
    ur = u.reshape(bsz, n, t_len, width).transpose(0, 2, 1, 3)
    blk = pl.BlockSpec((1, t_len, n, win), lambda b, k: (b, 0, 0, k))
    tab = lambda a: pl.BlockSpec((None, 1) + a.shape[2:], lambda b, k: (layer, k, 0, 0))
    g = pl.pallas_call(
        functools.partial(_ssm_kernel, t_len=t_len, rb=rb),
        grid=(bsz, nb),
        in_specs=[blk, tab(bb_t), tab(cc_t), tab(a_t), tab(d_t)],
        out_specs=blk,
        out_shape=jax.ShapeDtypeStruct((bsz, t_len, n, width), BF16),
        scratch_shapes=[pltpu.VMEM((t_len, n, sl2), F32), pltpu.VMEM((n, sl2), F32),
                        pltpu.VMEM((n, sl2), F32)],
        compiler_params=_cparams(("parallel", "arbitrary")),
        name="ssm",
    )(ur, bb_t, cc_t, a_t, d_t)
    return g.transpose(0, 2, 1, 3).reshape(bsz * seq, width)


def _mem_kernel(q_ref, k_ref, v_ref, o_ref):
    for h in range(MEM_HEADS):
        sl = slice(h * HEAD_DIM, (h + 1) * HEAD_DIM)
        s = lax.dot_general(q_ref[:, sl], k_ref[:, sl], (((1,), (1,)), ((), ())),
                            preferred_element_type=F32)
        p = jnp.exp(s - jnp.max(s, axis=-1, keepdims=True))
        l = jnp.sum(p, axis=-1, keepdims=True)
        o = jnp.dot(p.astype(BF16), v_ref[:, sl], preferred_element_type=F32)
        o_ref[:, sl] = (o / l).astype(o_ref.dtype)


def _mem_attention(proj, kv, bsz, seq, n_mem, q_col_block):
    tq = min(1024, seq)
    nq = seq // tq
    return pl.pallas_call(
        _mem_kernel,
        grid=(bsz, nq),
        in_specs=[
            pl.BlockSpec((tq, MEM_WIDTH), lambda b, i: (b * nq + i, q_col_block)),
            pl.BlockSpec((n_mem, MEM_WIDTH), lambda b, i: (b, 0)),
            pl.BlockSpec((n_mem, MEM_WIDTH), lambda b, i: (b, 1)),
        ],
        out_specs=pl.BlockSpec((tq, MEM_WIDTH), lambda b, i: (b * nq + i, 0)),
        out_shape=jax.ShapeDtypeStruct((bsz * seq, MEM_WIDTH), BF16),
        compiler_params=_cparams(("parallel", "arbitrary")),
        name="mem_attention",
    )(proj, kv, kv)


def _layer_norm(xf, g, b):
    mu = jnp.mean(xf, axis=-1, keepdims=True)
    xc = xf - mu
    var = jnp.mean(xc * xc, axis=-1, keepdims=True)
    return xc * lax.rsqrt(var + LN_EPS) * g + b


def _merge_kernel(sb_ref, g_ref, mm_ref, g0_ref, g1_ref, g2_ref, x_ref,
                  wsb_ref, wglu_ref, wsso_ref, wmem_ref, wo_ref, lng_ref, lnb_ref,
                  of_ref, ob_ref, *, alpha, ssm_width):
    p_sb = jnp.dot(sb_ref[...], wsb_ref[...], preferred_element_type=F32)
    glu = jnp.dot(g_ref[...], wglu_ref[...], preferred_element_type=F32)
    gated = glu[:, :ssm_width] * _sigmoid(glu[:, ssm_width:])
    p_ssm = jnp.dot(gated.astype(BF16), wsso_ref[...], preferred_element_type=F32)
    p_mem = jnp.dot(mm_ref[...], wmem_ref[...], preferred_element_type=F32)
    merged = (g0_ref[...].astype(F32) * p_sb + g1_ref[...].astype(F32) * p_ssm
              + g2_ref[...].astype(F32) * p_mem)
    mix = jnp.dot(merged.astype(BF16), wo_ref[...], preferred_element_type=F32)
    y = _layer_norm(alpha * x_ref[...] + mix, lng_ref[...], lnb_ref[...])
    of_ref[...] = y
    ob_ref[...] = y.astype(BF16)


def _layer_spec(a, layer):
    return pl.BlockSpec((None,) + a.shape[1:], lambda i: (layer,) + (0,) * (a.ndim - 1),
                        pipeline_mode=pl.Buffered(1))


def _merge(sb, g, mm, proj, x, wsb, wglu, wsso, wmem, wo, lng, lnb, layer, *, alpha, gate_col_block, tm):
    m, d = x.shape
    ssm_width = g.shape[1]
    row = lambda w: pl.BlockSpec((tm, w), lambda i: (i, 0))
    gate = lambda k: pl.BlockSpec((tm, d), lambda i: (i, gate_col_block + k))
    return pl.pallas_call(
        functools.partial(_merge_kernel, alpha=alpha, ssm_width=ssm_width),
        grid=(m // tm,),
        in_specs=[row(sb.shape[1]), row(ssm_width), row(mm.shape[1]), gate(0), gate(1), gate(2), row(d),
                  *[_layer_spec(a, layer) for a in (wsb, wglu, wsso, wmem, wo, lng, lnb)]],
        out_specs=[row(d), row(d)],
        out_shape=[jax.ShapeDtypeStruct((m, d), F32), jax.ShapeDtypeStruct((m, d), BF16)],
        compiler_params=_cparams(("parallel",)),
        name="merge",
    )(sb, g, mm, proj, proj, proj, x, wsb, wglu, wsso, wmem, wo, lng, lnb)


def _ffn_kernel(xb_ref, x_ref, wg_ref, wu_ref, wd_ref, lng_ref, lnb_ref, of_ref, ob_ref, *, alpha):
    j = pl.program_id(1)

    @pl.when(j == 0)
    def _():
        of_ref[...] = alpha * x_ref[...]

    xb = xb_ref[...]
    fg = jnp.dot(xb, wg_ref[...], preferred_element_type=F32)
    fu = jnp.dot(xb, wu_ref[...], preferred_element_type=F32)
    h = (fg * _sigmoid(fg) * fu).astype(BF16)
    of_ref[...] += jnp.dot(h, wd_ref[...], preferred_element_type=F32)

    @pl.when(j == pl.num_programs(1) - 1)
    def _():
        y = _layer_norm(of_ref[...], lng_ref[...], lnb_ref[...])
        of_ref[...] = y
        ob_ref[...] = y.astype(BF16)


def _ffn(xb, x, wgu, wd, lng, lnb, layer, *, alpha, tm, tf):
    m, d = x.shape
    d_ff = wd.shape[1]
    nf = d_ff // tf
    return pl.pallas_call(
        functools.partial(_ffn_kernel, alpha=alpha),
        grid=(m // tm, nf),
        in_specs=[
            pl.BlockSpec((tm, d), lambda i, j: (i, 0)),
            pl.BlockSpec((tm, d), lambda i, j: (i, 0)),
            pl.BlockSpec((None, d, tf), lambda i, j: (layer, 0, j)),
            pl.BlockSpec((None, d, tf), lambda i, j: (layer, 0, nf + j)),
            pl.BlockSpec((None, tf, d), lambda i, j: (layer, j, 0)),
            pl.BlockSpec((None, 1, d), lambda i, j: (layer, 0, 0)),
            pl.BlockSpec((None, 1, d), lambda i, j: (layer, 0, 0)),
        ],
        out_specs=[pl.BlockSpec((tm, d), lambda i, j: (i, 0)), pl.BlockSpec((tm, d), lambda i, j: (i, 0))],
        out_shape=[jax.ShapeDtypeStruct((m, d), F32), jax.ShapeDtypeStruct((m, d), BF16)],
        compiler_params=_cparams(("parallel", "arbitrary")),
        name="ffn",
    )(xb, x, wgu, wgu, wd, lng, lnb)


def _forward(x, mem, w_in, b_in, sb_w_out, ssm_lambda_re, ssm_lambda_im, ssm_log_dt,
             ssm_b_re, ssm_b_im, ssm_c_re, ssm_c_im, ssm_d, ssm_w_glu, ssm_w_out,
             mem_w_kv, mem_w_out, w_o, ln1_g, ln1_b, ffn_w_gate_up, ffn_w_down,
             ln2_g, ln2_b, *, alpha):
    bsz, seq, d = x.shape
    depth = w_in.shape[0]
    n_mem = mem.shape[1]
    in_width = w_in.shape[2]
    ssm_width = ssm_d.shape[1]
    m = bsz * seq
    tn = 2048
    tm = min(1024, m)
    q_scale = HEAD_DIM ** -0.5

    u_off = 3 * SB_WIDTH
    qm_off = u_off + ssm_width
    gate_off = qm_off + MEM_WIDTH
    assert gate_off % tn == 0 and gate_off % d == 0 and qm_off % MEM_WIDTH == 0
    assert in_width == gate_off + N_BRANCHES * d
    col = jnp.arange(in_width)
    scale_vec = jnp.where((col < SB_WIDTH) | ((col >= qm_off) & (col < gate_off)), q_scale, 1.0)
    scale_vec = scale_vec.astype(F32)

    tabs = _ssm_tables(ssm_lambda_re, ssm_lambda_im, ssm_log_dt, ssm_b_re, ssm_b_im,
                       ssm_c_re, ssm_c_im, ssm_d)
    bf = lambda a: a.astype(BF16)
    row3 = lambda a: a[:, None, :]
    w_in_b, wkv_b, wsb_b, wglu_b, wsso_b = bf(w_in * scale_vec), bf(mem_w_kv), bf(sb_w_out), bf(ssm_w_glu), bf(ssm_w_out)
    wmem_b, wo_b, wgu_b, wd_b = bf(mem_w_out), bf(w_o), bf(ffn_w_gate_up), bf(ffn_w_down)
    b_in3, ln1_g3, ln1_b3, ln2_g3, ln2_b3 = row3(b_in * scale_vec), row3(ln1_g), row3(ln1_b), row3(ln2_g), row3(ln2_b)
    mem_b = bf(mem.reshape(bsz * n_mem, d))
    kv_width = mem_w_kv.shape[2]
    kv_zero = jnp.zeros((depth, 1, kv_width), F32)

    xf = x.reshape(m, d)
    xb = xf
    for l in range(depth):
        proj = _proj(xb, w_in_b, b_in3, l, tm=tm, tn=tn, n_plain=gate_off // tn)
        sb = _sb_attention(proj, bsz, seq)
        g = _ssm_branch(proj[:, u_off:qm_off], tabs, l, bsz, seq)
        kv = _proj(mem_b, wkv_b, kv_zero, l, tm=min(1024, bsz * n_mem), tn=kv_width, n_plain=1)
        mm = _mem_attention(proj, kv, bsz, seq, n_mem, qm_off // MEM_WIDTH)
        xf, xb = _merge(sb, g, mm, proj, xf, wsb_b, wglu_b, wsso_b, wmem_b, wo_b, ln1_g3, ln1_b3, l,
                        alpha=alpha, gate_col_block=gate_off // d, tm=min(256, m))
        xf, xb = _ffn(xb, xf, wgu_b, wd_b, ln2_g3, ln2_b3, l, alpha=alpha, tm=min(512, m), tf=512)
    return xf.reshape(bsz, seq, d)


def kernel(x, mem, w_in, b_in, sb_w_out, ssm_lambda_re, ssm_lambda_im, ssm_log_dt, ssm_b_re, ssm_b_im, ssm_c_re, ssm_c_im, ssm_d, ssm_w_glu, ssm_w_out, mem_w_kv, mem_w_out, w_o, ln1_g, ln1_b, ffn_w_gate_up, ffn_w_down, ln2_g, ln2_b):
    depth = w_in.shape[0]
    alpha = (2 * depth) ** 0.25
    return _forward(x, mem, w_in, b_in, sb_w_out, ssm_lambda_re, ssm_lambda_im, ssm_log_dt,
                    ssm_b_re, ssm_b_im, ssm_c_re, ssm_c_im, ssm_d, ssm_w_glu, ssm_w_out,
                    mem_w_kv, mem_w_out, w_o, ln1_g, ln1_b, ffn_w_gate_up, ffn_w_down,
                    ln2_g, ln2_b, alpha=alpha)
```

```python
import functools
import math

import jax
import jax.numpy as jnp
from jax import lax
from jax.experimental import pallas as pl
from jax.experimental.pallas import tpu as pltpu

F32 = jnp.float32
BF16 = jnp.bfloat16

SB_HEADS = 8
HEAD_DIM = 128
SB_WIDTH = SB_HEADS * HEAD_DIM
SSM_GROUP = 16
SSM_STATE = 64
MEM_HEADS = 4
MEM_WIDTH = MEM_HEADS * HEAD_DIM
N_BRANCHES = 3
LN_EPS = 1e-5

LANES = 128

SSM_CHUNK = 16
SSM_ROW_BLOCK = 16
SSM_TIME_GROUP = 4
SSM_GROUPS_PER_STEP = LANES // SSM_GROUP

SB_TQ = 256
SB_TK = 128
SB_LOG_CUTOFF = -110.0
SB_STATIC_TILES = 2
SB_HEADS_PER_STEP = 4

VMEM_LIMIT = 56 * 1024 * 1024


def _cparams(sem):
    return pltpu.CompilerParams(dimension_semantics=sem, vmem_limit_bytes=VMEM_LIMIT)


def _sigmoid(x):
    return 0.5 * jnp.tanh(0.5 * x) + 0.5


def _proj_kernel(x_ref, w_ref, b_ref, o_ref, *, n_plain):
    acc = jnp.dot(x_ref[...].astype(BF16), w_ref[...], preferred_element_type=F32)
    acc = acc + b_ref[...]
    is_gate = pl.program_id(1) >= n_plain
    o_ref[...] = jnp.where(is_gate, 0.5 * jnp.tanh(acc) + 0.5, acc).astype(o_ref.dtype)


def _proj(x, w, b, layer, *, tm, tn, n_plain):
    m, k = x.shape
    n = w.shape[2]
    return pl.pallas_call(
        functools.partial(_proj_kernel, n_plain=n_plain),
        grid=(m // tm, n // tn),
        in_specs=[
            pl.BlockSpec((tm, k), lambda i, j: (i, 0)),
            pl.BlockSpec((None, k, tn), lambda i, j: (layer, 0, j)),
            pl.BlockSpec((None, 1, tn), lambda i, j: (layer, 0, j)),
        ],
        out_specs=pl.BlockSpec((tm, tn), lambda i, j: (i, j)),
        out_shape=jax.ShapeDtypeStruct((m, n), BF16),
        compiler_params=_cparams(("parallel", "arbitrary")),
        name="proj",
    )(x, w, b)


def _sb_kernel(q_ref, k_ref, v_ref, tri_ref, o_ref, acc_ref, carry_ref, z_ref, cs_ref, *, tq, tk, n_static):
    nq = q_ref.shape[0] // tq
    n_diag = tq // tk
    heads = range(q_ref.shape[1] // HEAD_DIM)
    lanes = lambda h: slice(h * HEAD_DIM, (h + 1) * HEAD_DIM)

    def tile(q, h, j, carry):
        start = pl.multiple_of(j * tk, tk)
        k = k_ref[pl.ds(start, tk), lanes(h)]
        v = v_ref[pl.ds(start, tk), lanes(h)]
        z = lax.dot_general(q, k, (((1,), (1,)), ((), ())), preferred_element_type=F32)
        sp = jnp.maximum(z, 0.0) + jnp.log(1.0 + jnp.exp(-jnp.abs(z)))
        cs = jnp.dot(sp.astype(BF16), tri_ref[...], preferred_element_type=F32)
        incl = cs[:, :tk]
        tot = cs[:, tk:]
        w = jnp.exp(z + incl + carry)
        return jnp.dot(w.astype(BF16), v, preferred_element_type=F32), carry + tot

    def causal_mask(w, n_st):
        d = w - n_st
        if d < 0:
            return None
        row = lax.broadcasted_iota(jnp.int32, (tq, tk), 0)
        col = lax.broadcasted_iota(jnp.int32, (tq, tk), 1)
        return col + d * tk < row

    def window_scores(i, h, n_st):
        n_win = n_st + n_diag
        q = q_ref[pl.ds(pl.multiple_of(i * tq, tq), tq), lanes(h)]
        start = pl.multiple_of((i * n_diag - n_st) * tk, tk)
        kw = k_ref[pl.ds(start, n_win * tk), lanes(h)]
        z = lax.dot_general(q, kw, (((1,), (1,)), ((), ())), preferred_element_type=F32)
        hl = []
        for w in range(n_win):
            zw = z[:, w * tk:(w + 1) * tk]
            sp = jnp.maximum(zw, 0.0) + jnp.log(1.0 + jnp.exp(-jnp.abs(zw)))
            cw = causal_mask(w, n_st)
            if cw is not None:
                sp = jnp.where(cw, sp, 0.0)
            hl.append(sp.astype(BF16))
        cs = jnp.dot(jnp.concatenate(hl, axis=0), tri_ref[...], preferred_element_type=F32)
        return z, cs

    def window_output(i, h, n_st, z, cs):
        n_win = n_st + n_diag
        start = pl.multiple_of((i * n_diag - n_st) * tk, tk)
        vw = v_ref[pl.ds(start, n_win * tk), lanes(h)]
        carry = jnp.zeros((tq, tk), F32)
        ws = [None] * n_win
        for w in reversed(range(n_win)):
            incl = cs[w * tq:(w + 1) * tq, :tk]
            tot = cs[w * tq:(w + 1) * tq, tk:]
            ww = jnp.exp(z[:, w * tk:(w + 1) * tk] + incl + carry)
            cw = causal_mask(w, n_st)
            if cw is not None:
                ww = jnp.where(cw, ww, 0.0)
            ws[w] = ww.astype(BF16)
            carry = carry + tot
        acc = jnp.dot(jnp.concatenate(ws, axis=1), vw, preferred_element_type=F32)
        return acc, carry

    def qtile(i, n_st, pipelined):
        row0 = pl.multiple_of(i * tq, tq)
        carry = []
        for h in heads:
            if pipelined:
                z, cs = z_ref[h], cs_ref[h]
            else:
                z, cs = window_scores(i, h, n_st)
            acc_h, carry_h = window_output(i, h, n_st, z, cs)
            acc_ref[h] = acc_h
            carry_ref[h] = carry_h
            carry.append(carry_h)
        if pipelined:
            nxt = jnp.minimum(i + 1, nq - 1)
            for h in heads:
                z_ref[h], cs_ref[h] = window_scores(nxt, h, n_st)

        def cond(c):
            j, m = c
            return jnp.logical_and(j >= 0, m > SB_LOG_CUTOFF)

        def body(c):
            j, _ = c
            m = None
            for h in heads:
                pv, new_carry = tile(q_ref[pl.ds(row0, tq), lanes(h)], h, j, carry_ref[h])
                acc_ref[h] += pv
                carry_ref[h] = new_carry
                mh = jnp.max(new_carry)
                m = mh if m is None else jnp.maximum(m, mh)
            return j - 1, m

        m0 = functools.reduce(jnp.maximum, [jnp.max(c) for c in carry])
        lax.while_loop(cond, body, (i * n_diag - 1 - n_st, m0))
        for h in heads:
            o_ref[pl.ds(row0, tq), lanes(h)] = acc_ref[h].astype(o_ref.dtype)

    qtile(0, 0, False)
    if nq > 1:
        n_st = n_static
        for h in heads:
            z_ref[h], cs_ref[h] = window_scores(1, h, n_st)

        def qbody(i, c):
            qtile(i, n_st, True)
            return c

        lax.fori_loop(1, nq, qbody, 0)


def _sb_prefix_matrix(tk):
    j = jnp.arange(tk)[:, None]
    s = jnp.arange(tk)[None, :]
    return -jnp.concatenate([(j >= s).astype(F32), jnp.ones((tk, tk), F32)], axis=1).astype(BF16)


def _sb_attention(proj, bsz, seq):
    tq, tk = min(SB_TQ, seq), min(SB_TK, seq)
    tri = _sb_prefix_matrix(tk)
    n_static = min(SB_STATIC_TILES, tq // tk)
    n_win = n_static + tq // tk
    hp = SB_HEADS_PER_STEP
    ng = SB_HEADS // hp
    col = lambda off: pl.BlockSpec((seq, hp * HEAD_DIM), lambda b, h: (b, off + h))
    return pl.pallas_call(
        functools.partial(_sb_kernel, tq=tq, tk=tk, n_static=n_static),
        grid=(bsz, ng),
        in_specs=[col(0), col(ng), col(2 * ng),
                  pl.BlockSpec((tk, 2 * tk), lambda b, h: (0, 0))],
        out_specs=pl.BlockSpec((seq, hp * HEAD_DIM), lambda b, h: (b, h)),
        out_shape=jax.ShapeDtypeStruct((bsz * seq, SB_WIDTH), BF16),
        scratch_shapes=[pltpu.VMEM((hp, tq, HEAD_DIM), F32), pltpu.VMEM((hp, tq, tk), F32),
                        pltpu.VMEM((hp, tq, n_win * tk), F32), pltpu.VMEM((hp, n_win * tq, 2 * tk), F32)],
        compiler_params=_cparams(("parallel", "arbitrary")),
        name="sb_attention",
    )(proj, proj, proj, tri)


def _gelu_tanh(x):
    c = math.sqrt(2.0 / math.pi)
    return 0.5 * x * (1.0 + jnp.tanh(c * (x + 0.044715 * (x * x * x))))


def _ssm_kernel(u_ref, bb_ref, cc_ref, a_ref, d_ref, o_ref, x_ref, s_ref, hp_ref, *, t_len, rb):
    n = u_ref.shape[2]
    sl = a_ref.shape[-1]
    a_re = jnp.broadcast_to(a_ref[0, 0:1, :], (rb, sl))
    a_im = jnp.broadcast_to(a_ref[0, 1:2, :], (rb, sl))
    row_blocks = [slice(r * rb, (r + 1) * rb) for r in range(n // rb)]

    tg = SSM_TIME_GROUP

    def advance(t0, first, keep):
        for rows in row_blocks:
            if first and keep:
                h_re, h_im = hp_ref[rows, 0:sl], hp_ref[rows, sl:2 * sl]
            elif first:
                h_re = h_im = jnp.zeros((rb, sl), F32)
            elif keep:
                h_re, h_im = x_ref[t0 - 1, rows, 0:sl], x_ref[t0 - 1, rows, sl:2 * sl]
            else:
                h_re, h_im = s_ref[rows, 0:sl], s_ref[rows, sl:2 * sl]
            for t in range(t0, t0 + tg):
                x_re, x_im = x_ref[t, rows, 0:sl], x_ref[t, rows, sl:2 * sl]
                h_re, h_im = (a_re * h_re - a_im * h_im + x_re, a_re * h_im + a_im * h_re + x_im)
                if keep:
                    x_ref[t, rows, 0:sl] = h_re
                    x_ref[t, rows, sl:2 * sl] = h_im
            if not keep:
                s_ref[rows, 0:sl] = h_re
                s_ref[rows, sl:2 * sl] = h_im

    for t0 in range(0, t_len, tg):
        for t in range(t0, t0 + tg):
            x_ref[t] = jnp.dot(u_ref[0, t], bb_ref[0], preferred_element_type=F32)
        advance(t0, t0 == 0, False)

    at_re, at_im = a_ref[0, 2:3, :], a_ref[0, 3:4, :]

    def chunk_scan(i, h):
        h_re, h_im = h
        row = pl.ds(i, 1)
        hp_ref[row, 0:sl] = h_re
        hp_ref[row, sl:2 * sl] = h_im
        s_re, s_im = s_ref[row, 0:sl], s_ref[row, sl:2 * sl]
        return (at_re * h_re - at_im * h_im + s_re, at_re * h_im + at_im * h_re + s_im)

    zero = jnp.zeros((1, sl), F32)
    lax.fori_loop(0, n, chunk_scan, (zero, zero), unroll=8)

    for t0 in range(0, t_len, tg):
        advance(t0, t0 == 0, True)
        for t in range(t0, t0 + tg):
            y = jnp.dot(x_ref[t].astype(BF16), cc_ref[0], preferred_element_type=F32)
            y = y + d_ref[0] * u_ref[0, t].astype(F32)
            o_ref[0, t] = _gelu_tanh(y).astype(o_ref.dtype)


def _ssm_tables(lam_re, lam_im, log_dt, b_re, b_im, c_re, c_im, d_skip):
    t_len = SSM_CHUNK
    depth, groups, pst = lam_re.shape
    gc, gp = SSM_GROUP, SSM_GROUPS_PER_STEP
    nb = groups // gp
    dt = jnp.exp(log_dt)[..., None]
    ar, ai = lam_re * dt, lam_im * dt
    a_re, a_im = jnp.exp(ar) * jnp.cos(ai), jnp.exp(ar) * jnp.sin(ai)
    at_re = jnp.exp(t_len * ar) * jnp.cos(t_len * ai)
    at_im = jnp.exp(t_len * ar) * jnp.sin(t_len * ai)
    den = lam_re * lam_re + lam_im * lam_im
    nr, ni = a_re - 1.0, a_im
    f_re = (nr * lam_re + ni * lam_im) / den
    f_im = (ni * lam_re - nr * lam_im) / den
    bb_re = f_re[..., None] * b_re - f_im[..., None] * b_im
    bb_im = f_re[..., None] * b_im + f_im[..., None] * b_re
    eye = jnp.eye(gp, dtype=F32)

    def in_block(m):
        m = m.reshape(depth, nb, gp, pst, gc).transpose(0, 1, 2, 4, 3)
        return (m[:, :, :, :, None, :] * eye[None, None, :, None, :, None]).reshape(depth, nb, gp * gc, gp * pst)

    def out_block(m):
        m = m.reshape(depth, nb, gp, gc, pst).transpose(0, 1, 2, 4, 3)
        return (m[:, :, :, :, None, :] * eye[None, None, :, None, :, None]).reshape(depth, nb, gp * pst, gp * gc)

    bb_t = jnp.concatenate([in_block(bb_re), in_block(bb_im)], axis=3).astype(BF16)
    cc_t = jnp.concatenate([out_block(c_re), out_block(-c_im)], axis=2).astype(BF16)
    lanes = lambda m: m.reshape(depth, nb, gp * pst)
    a_t = jnp.stack([lanes(a_re), lanes(a_im), lanes(at_re), lanes(at_im)], axis=2)
    d_t = d_skip.reshape(depth, nb, 1, gp * gc)
    return bb_t, cc_t, a_t, d_t


def _ssm_branch(u, tabs, layer, bsz, seq):
    bb_t, cc_t, a_t, d_t = tabs
    _, nb, win, sl2 = bb_t.shape
    t_len = SSM_CHUNK
    n = seq // t_len
    rb = min(SSM_ROW_BLOCK, n)
    width = u.shape[1]
    ur = u.reshape(bsz, n, t_len, width).transpose(0, 2, 1, 3)
    blk = pl.BlockSpec((1, t_len, n, win), lambda b, k: (b, 0, 0, k))
    tab = lambda a: pl.BlockSpec((None, 1) + a.shape[2:], lambda b, k: (layer, k, 0, 0))
    g = pl.pallas_call(
        functools.partial(_ssm_kernel, t_len=t_len, rb=rb),
        grid=(bsz, nb),
        in_specs=[blk, tab(bb_t), tab(cc_t), tab(a_t), tab(d_t)],
        out_specs=blk,
        out_shape=jax.ShapeDtypeStruct((bsz, t_len, n, width), BF16),
        scratch_shapes=[pltpu.VMEM((t_len, n, sl2), F32), pltpu.VMEM((n, sl2), F32),
                        pltpu.VMEM((n, sl2), F32)],
        compiler_params=_cparams(("parallel", "arbitrary")),
        name="ssm",
    )(ur, bb_t, cc_t, a_t, d_t)
    return g.transpose(0, 2, 1, 3).reshape(bsz * seq, width)


def _mem_kernel(q_ref, k_ref, v_ref, o_ref):
    for h in range(MEM_HEADS):
        sl = slice(h * HEAD_DIM, (h + 1) * HEAD_DIM)
        s = lax.dot_general(q_ref[:, sl], k_ref[:, sl], (((1,), (1,)), ((), ())),
                            preferred_element_type=F32)
        p = jnp.exp(s - jnp.max(s, axis=-1, keepdims=True))
        l = jnp.sum(p, axis=-1, keepdims=True)
        o = jnp.dot(p.astype(BF16), v_ref[:, sl], preferred_element_type=F32)
        o_ref[:, sl] = (o / l).astype(o_ref.dtype)


def _mem_attention(proj, kv, bsz, seq, n_mem, q_col_block):
    tq = min(1024, seq)
    nq = seq // tq
    return pl.pallas_call(
        _mem_kernel,
        grid=(bsz, nq),
        in_specs=[
            pl.BlockSpec((tq, MEM_WIDTH), lambda b, i: (b * nq + i, q_col_block)),
            pl.BlockSpec((n_mem, MEM_WIDTH), lambda b, i: (b, 0)),
            pl.BlockSpec((n_mem, MEM_WIDTH), lambda b, i: (b, 1)),
        ],
        out_specs=pl.BlockSpec((tq, MEM_WIDTH), lambda b, i: (b * nq + i, 0)),
        out_shape=jax.ShapeDtypeStruct((bsz * seq, MEM_WIDTH), BF16),
        compiler_params=_cparams(("parallel", "arbitrary")),
        name="mem_attention",
    )(proj, kv, kv)


def _layer_norm(xf, g, b):
    mu = jnp.mean(xf, axis=-1, keepdims=True)
    xc = xf - mu
    var = jnp.mean(xc * xc, axis=-1, keepdims=True)
    return xc * lax.rsqrt(var + LN_EPS) * g + b


def _merge_kernel(sb_ref, g_ref, mm_ref, g0_ref, g1_ref, g2_ref, x_ref,
                  wsb_ref, wglu_ref, wsso_ref, wmem_ref, wo_ref, lng_ref, lnb_ref,
                  of_ref, ob_ref, *, alpha, ssm_width):
    p_sb = jnp.dot(sb_ref[...], wsb_ref[...], preferred_element_type=F32)
    glu = jnp.dot(g_ref[...], wglu_ref[...], preferred_element_type=F32)
    gated = glu[:, :ssm_width] * _sigmoid(glu[:, ssm_width:])
    p_ssm = jnp.dot(gated.astype(BF16), wsso_ref[...], preferred_element_type=F32)
    p_mem = jnp.dot(mm_ref[...], wmem_ref[...], preferred_element_type=F32)
    merged = (g0_ref[...].astype(F32) * p_sb + g1_ref[...].astype(F32) * p_ssm
              + g2_ref[...].astype(F32) * p_mem)
    mix = jnp.dot(merged.astype(BF16), wo_ref[...], preferred_element_type=F32)
    y = _layer_norm(alpha * x_ref[...] + mix, lng_ref[...], lnb_ref[...])
    of_ref[...] = y
    ob_ref[...] = y.astype(BF16)


def _layer_spec(a, layer):
    return pl.BlockSpec((None,) + a.shape[1:], lambda i: (layer,) + (0,) * (a.ndim - 1),
                        pipeline_mode=pl.Buffered(1))


def _merge(sb, g, mm, proj, x, wsb, wglu, wsso, wmem, wo, lng, lnb, layer, *, alpha, gate_col_block, tm):
    m, d = x.shape
    ssm_width = g.shape[1]
    row = lambda w: pl.BlockSpec((tm, w), lambda i: (i, 0))
    gate = lambda k: pl.BlockSpec((tm, d), lambda i: (i, gate_col_block + k))
    return pl.pallas_call(
        functools.partial(_merge_kernel, alpha=alpha, ssm_width=ssm_width),
        grid=(m // tm,),
        in_specs=[row(sb.shape[1]), row(ssm_width), row(mm.shape[1]), gate(0), gate(1), gate(2), row(d),
                  *[_layer_spec(a, layer) for a in (wsb, wglu, wsso, wmem, wo, lng, lnb)]],
        out_specs=[row(d), row(d)],
        out_shape=[jax.ShapeDtypeStruct((m, d), F32), jax.ShapeDtypeStruct((m, d), BF16)],
        compiler_params=_cparams(("parallel",)),
        name="merge",
    )(sb, g, mm, proj, proj, proj, x, wsb, wglu, wsso, wmem, wo, lng, lnb)


def _ffn_kernel(xb_ref, x_ref, wg_ref, wu_ref, wd_ref, lng_ref, lnb_ref, of_ref, ob_ref, *, alpha):
    j = pl.program_id(1)

    @pl.when(j == 0)
    def _():
        of_ref[...] = alpha * x_ref[...]

    xb = xb_ref[...]
    fg = jnp.dot(xb, wg_ref[...], preferred_element_type=F32)
    fu = jnp.dot(xb, wu_ref[...], preferred_element_type=F32)
    h = (fg * _sigmoid(fg) * fu).astype(BF16)
    of_ref[...] += jnp.dot(h, wd_ref[...], preferred_element_type=F32)

    @pl.when(j == pl.num_programs(1) - 1)
    def _():
        y = _layer_norm(of_ref[...], lng_ref[...], lnb_ref[...])
        of_ref[...] = y
        ob_ref[...] = y.astype(BF16)


def _ffn(xb, x, wgu, wd, lng, lnb, layer, *, alpha, tm, tf):
    m, d = x.shape
    d_ff = wd.shape[1]
    nf = d_ff // tf
    return pl.pallas_call(
        functools.partial(_ffn_kernel, alpha=alpha),
        grid=(m // tm, nf),
        in_specs=[
            pl.BlockSpec((tm, d), lambda i, j: (i, 0)),
            pl.BlockSpec((tm, d), lambda i, j: (i, 0)),
            pl.BlockSpec((None, d, tf), lambda i, j: (layer, 0, j)),
            pl.BlockSpec((None, d, tf), lambda i, j: (layer, 0, nf + j)),
            pl.BlockSpec((None, tf, d), lambda i, j: (layer, j, 0)),
            pl.BlockSpec((None, 1, d), lambda i, j: (layer, 0, 0)),
            pl.BlockSpec((None, 1, d), lambda i, j: (layer, 0, 0)),
        ],
        out_specs=[pl.BlockSpec((tm, d), lambda i, j: (i, 0)), pl.BlockSpec((tm, d), lambda i, j: (i, 0))],
        out_shape=[jax.ShapeDtypeStruct((m, d), F32), jax.ShapeDtypeStruct((m, d), BF16)],
        compiler_params=_cparams(("parallel", "arbitrary")),
        name="ffn",
    )(xb, x, wgu, wgu, wd, lng, lnb)


def _forward(x, mem, w_in, b_in, sb_w_out, ssm_lambda_re, ssm_lambda_im, ssm_log_dt,
             ssm_b_re, ssm_b_im, ssm_c_re, ssm_c_im, ssm_d, ssm_w_glu, ssm_w_out,
             mem_w_kv, mem_w_out, w_o, ln1_g, ln1_b, ffn_w_gate_up, ffn_w_down,
             ln2_g, ln2_b, *, alpha):
    bsz, seq, d = x.shape
    depth = w_in.shape[0]
    n_mem = mem.shape[1]
    in_width = w_in.shape[2]
    ssm_width = ssm_d.shape[1]
    m = bsz * seq
    tn = 2048
    tm = min(1024, m)
    q_scale = HEAD_DIM ** -0.5

    u_off = 3 * SB_WIDTH
    qm_off = u_off + ssm_width
    gate_off = qm_off + MEM_WIDTH
    assert gate_off % tn == 0 and gate_off % d == 0 and qm_off % MEM_WIDTH == 0
    assert in_width == gate_off + N_BRANCHES * d
    col = jnp.arange(in_width)
    scale_vec = jnp.where((col < SB_WIDTH) | ((col >= qm_off) & (col < gate_off)), q_scale, 1.0)
    scale_vec = jnp.where(col >= gate_off, 0.5, scale_vec).astype(F32)

    tabs = _ssm_tables(ssm_lambda_re, ssm_lambda_im, ssm_log_dt, ssm_b_re, ssm_b_im,
                       ssm_c_re, ssm_c_im, ssm_d)
    bf = lambda a: a.astype(BF16)
    row3 = lambda a: a[:, None, :]
    w_in_b, wkv_b, wsb_b, wglu_b, wsso_b = bf(w_in * scale_vec), bf(mem_w_kv), bf(sb_w_out), bf(ssm_w_glu), bf(ssm_w_out)
    wmem_b, wo_b, wgu_b, wd_b = bf(mem_w_out), bf(w_o), bf(ffn_w_gate_up), bf(ffn_w_down)
    b_in3, ln1_g3, ln1_b3, ln2_g3, ln2_b3 = row3(b_in * scale_vec), row3(ln1_g), row3(ln1_b), row3(ln2_g), row3(ln2_b)
    mem_b = bf(mem.reshape(bsz * n_mem, d))
    kv_width = mem_w_kv.shape[2]
    kv_zero = jnp.zeros((depth, 1, kv_width), F32)

    xf = x.reshape(m, d)
    xb = xf
    for l in range(depth):
        proj = _proj(xb, w_in_b, b_in3, l, tm=tm, tn=tn, n_plain=gate_off // tn)
        sb = _sb_attention(proj, bsz, seq)
        g = _ssm_branch(proj[:, u_off:qm_off], tabs, l, bsz, seq)
        kv = _proj(mem_b, wkv_b, kv_zero, l, tm=min(1024, bsz * n_mem), tn=kv_width, n_plain=1)
        mm = _mem_attention(proj, kv, bsz, seq, n_mem, qm_off // MEM_WIDTH)
        xf, xb = _merge(sb, g, mm, proj, xf, wsb_b, wglu_b, wsso_b, wmem_b, wo_b, ln1_g3, ln1_b3, l,
                        alpha=alpha, gate_col_block=gate_off // d, tm=min(256, m))
        xf, xb = _ffn(xb, xf, wgu_b, wd_b, ln2_g3, ln2_b3, l, alpha=alpha, tm=min(512, m), tf=512)
    return xf.reshape(bsz, seq, d)


def kernel(x, mem, w_in, b_in, sb_w_out, ssm_lambda_re, ssm_lambda_im, ssm_log_dt, ssm_b_re, ssm_b_im, ssm_c_re, ssm_c_im, ssm_d, ssm_w_glu, ssm_w_out, mem_w_kv, mem_w_out, w_o, ln1_g, ln1_b, ffn_w_gate_up, ffn_w_down, ln2_g, ln2_b):
    depth = w_in.shape[0]
    alpha = (2 * depth) ** 0.25
    return _forward(x, mem, w_in, b_in, sb_w_out, ssm_lambda_re, ssm_lambda_im, ssm_log_dt,
                    ssm_b_re, ssm_b_im, ssm_c_re, ssm_c_im, ssm_d, ssm_w_glu, ssm_w_out,
                    mem_w_kv, mem_w_out, w_o, ln1_g, ln1_b, ffn_w_gate_up, ffn_w_down,
                    ln2_g, ln2_b, alpha=alpha)
```
